```python
import math
import jax, jax.numpy as jnp
from jax import lax
import numpy as np

D_MODEL = 1024
BATCH = 4
SEQ = 4096
DEPTH = 1

GRID_W = 64
HEAD_DIM = 64
A_HEADS = 8
A_KV = 2
B_HEADS = 8
B_KV = 2
Q_BLOCK = 128
WINDOW = 128
ROPE_THETA = 10000.0
N_BUCKETS = 32
MAX_DISTANCE = 128
PEER_HEADS = 8
PEER_NKEYS = 128
PEER_EXPERTS = PEER_NKEYS * PEER_NKEYS
PEER_TOPK = 16
PEER_DQ = 256
PEER_CHUNK = 128
EPS = 1e-6
NEG = -1e30

A_QW = A_HEADS * HEAD_DIM
A_KVW = A_KV * HEAD_DIM
B_QW = B_HEADS * HEAD_DIM
B_KVW = B_KV * HEAD_DIM
IN_WIDTHS = [A_QW, A_KVW, A_KVW, B_QW, B_KVW, B_KVW, D_MODEL, D_MODEL]
IN_WIDTH = sum(IN_WIDTHS)
IN_OFFSETS = [int(o) for o in np.cumsum(IN_WIDTHS)[:-1]]

kernel_name = "hybrid_gated_axial_window_peer"


def rmsnorm(x, g):
    xf = x.astype(jnp.float32)
    y = xf * lax.rsqrt(jnp.mean(xf * xf, axis=-1, keepdims=True) + EPS)
    return (y * g.astype(jnp.float32)).astype(x.dtype)


def rope1d(u, pos):
    half = u.shape[-1] // 2
    freqs = ROPE_THETA ** (-jnp.arange(half, dtype=jnp.float32) / half)
    ang = pos.astype(jnp.float32)[:, None] * freqs[None, :]
    cos = jnp.cos(ang)[None, :, None, :].astype(u.dtype)
    sin = jnp.sin(ang)[None, :, None, :].astype(u.dtype)
    u1, u2 = u[..., :half], u[..., half:]
    return jnp.concatenate([u1 * cos - u2 * sin, u2 * cos + u1 * sin], axis=-1)


def axial_rope(t, row, col):
    rd = t.shape[-1] // 2
    return jnp.concatenate([rope1d(t[..., :rd], row), rope1d(t[..., rd:], col)], axis=-1)


def dense_gqa_blocks(q, k, v):
    bsz, s = q.shape[0], q.shape[1]
    nb = s // Q_BLOCK
    r = A_HEADS // A_KV
    scale = HEAD_DIM ** -0.5
    qb = q.reshape(bsz, nb, Q_BLOCK, A_KV, r, HEAD_DIM).transpose(1, 0, 2, 3, 4, 5)

    def one_block(qblk):
        sc = jnp.einsum('bqgrd,bkgd->bgrqk', qblk, k).astype(jnp.float32) * scale
        p = jax.nn.softmax(sc, axis=-1)
        return jnp.einsum('bgrqk,bkgd->bqgrd', p.astype(v.dtype), v)

    o = lax.map(one_block, qb)
    return o.transpose(1, 0, 2, 3, 4, 5).reshape(bsz, s, A_HEADS * HEAD_DIM)


def t5_bucket(rel):
    nbk = N_BUCKETS // 2
    max_exact = nbk // 2
    ret = jnp.where(rel > 0, nbk, 0)
    n = jnp.abs(rel)
    large = max_exact + (jnp.log(jnp.maximum(n, 1).astype(jnp.float32) / max_exact)
                         / math.log(MAX_DISTANCE / max_exact) * (nbk - max_exact)).astype(jnp.int32)
    large = jnp.minimum(large, nbk - 1)
    return ret + jnp.where(n < max_exact, n, large)


def windowed_gqa_sink(q, k, v, rel_bias, sink):
    bsz, s = q.shape[0], q.shape[1]
    nb = s // Q_BLOCK
    r = B_HEADS // B_KV
    scale = HEAD_DIM ** -0.5
    pad = ((0, 0), (Q_BLOCK, Q_BLOCK), (0, 0), (0, 0))
    kb = jnp.pad(k, pad).reshape(bsz, nb + 2, Q_BLOCK, B_KV, HEAD_DIM)
    vb = jnp.pad(v, pad).reshape(bsz, nb + 2, Q_BLOCK, B_KV, HEAD_DIM)
    kwin = jnp.concatenate([kb[:, :-2], kb[:, 1:-1], kb[:, 2:]], axis=2)
    vwin = jnp.concatenate([vb[:, :-2], vb[:, 1:-1], vb[:, 2:]], axis=2)
    qb = q.reshape(bsz, nb, Q_BLOCK, B_KV, r, HEAD_DIM)

    qi = jnp.arange(Q_BLOCK)
    kj = jnp.arange(3 * Q_BLOCK)
    rel = kj[None, :] - Q_BLOCK - qi[:, None]
    bias = rel_bias[t5_bucket(rel)].astype(jnp.float32)
    bias = bias.transpose(2, 0, 1).reshape(B_KV, r, Q_BLOCK, 3 * Q_BLOCK)
    kpos = jnp.arange(nb)[:, None] * Q_BLOCK - Q_BLOCK + kj[None, :]
    inb = (kpos >= 0) & (kpos < s)
    mask = inb[:, None, :] & (jnp.abs(rel) <= WINDOW)[None]

    sc = jnp.einsum('bnqgrd,bnkgd->bngrqk', qb, kwin).astype(jnp.float32) * scale + bias[None, None]
    sc = jnp.where(mask[None, :, None, None], sc, NEG)
    sink_l = sink.astype(jnp.float32).reshape(B_KV, r, 1, 1)
    m = jnp.maximum(jnp.max(sc, axis=-1, keepdims=True), sink_l)
    p = jnp.exp(sc - m)
    p = p / (jnp.sum(p, axis=-1, keepdims=True) + jnp.exp(sink_l - m))
    o = jnp.einsum('bngrqk,bnkgd->bnqgrd', p.astype(v.dtype), vwin)
    return o.reshape(bsz, s, B_HEADS * HEAD_DIM)


def peer_ffn(xn, wq, subkeys, u_tab, v_tab):
    bsz, s, d = xn.shape
    q = (xn @ wq).reshape(bsz, s, PEER_HEADS, 2, PEER_DQ // 2)
    sc = jnp.einsum('bshcd,hcnd->bshcn', q, subkeys).astype(jnp.float32)
    s1, i1 = lax.top_k(sc[..., 0, :], PEER_TOPK)
    s2, i2 = lax.top_k(sc[..., 1, :], PEER_TOPK)
    cand = (s1[..., :, None] + s2[..., None, :]).reshape(bsz, s, PEER_HEADS, PEER_TOPK * PEER_TOPK)
    top_s, ci = lax.top_k(cand, PEER_TOPK)
    e_idx = (jnp.take_along_axis(i1, ci // PEER_TOPK, axis=-1) * PEER_NKEYS
             + jnp.take_along_axis(i2, ci % PEER_TOPK, axis=-1))
    gates = jax.nn.softmax(top_s, axis=-1).astype(xn.dtype)

    t = bsz * s
    nc = t // PEER_CHUNK
    xc = xn.reshape(nc, PEER_CHUNK, d)
    ec = e_idx.reshape(nc, PEER_CHUNK, PEER_HEADS, PEER_TOPK)
    gc = gates.reshape(nc, PEER_CHUNK, PEER_HEADS, PEER_TOPK)

    def one_chunk(args):
        xk, ek, gk = args
        u = jnp.take(u_tab, ek, axis=0)
        a = jnp.einsum('cd,chkd->chk', xk, u)
        w = gk * jax.nn.gelu(a, approximate=False)
        vv = jnp.take(v_tab, ek, axis=0)
        return jnp.einsum('chk,chkd->cd', w, vv)

    out = lax.map(one_chunk, (xc, ec, gc))
    return out.reshape(bsz, s, d)


def setup_inputs(seed: int = 0) -> dict:
    key = jax.random.key(seed)
    ks = jax.random.split(key, 20)
    f32 = jnp.float32
    nrm = lambda k, shp, sc: jax.random.normal(k, shp, f32) * sc
    return {
        'x': nrm(ks[0], (BATCH, SEQ, D_MODEL), 1.0),
        'g_mix': 1.0 + nrm(ks[1], (DEPTH, D_MODEL), 0.02),
        'w_in': nrm(ks[2], (DEPTH, D_MODEL, IN_WIDTH), D_MODEL ** -0.5),
        'b_gate': nrm(ks[3], (DEPTH, 2 * D_MODEL), 0.02),
        'q_norm_g': 1.0 + nrm(ks[4], (DEPTH, HEAD_DIM), 0.02),
        'k_norm_g': 1.0 + nrm(ks[5], (DEPTH, HEAD_DIM), 0.02),
        'rel_bias': nrm(ks[6], (N_BUCKETS, B_HEADS), 0.1),
        'sink': nrm(ks[7], (DEPTH, B_HEADS), 0.5),
        'w_branch_a': nrm(ks[8], (DEPTH, A_QW, D_MODEL), A_QW ** -0.5),
        'w_branch_b': nrm(ks[9], (DEPTH, B_QW, D_MODEL), B_QW ** -0.5),
        'w_out': nrm(ks[10], (DEPTH, D_MODEL, D_MODEL), D_MODEL ** -0.5),
        'g_ffn': 1.0 + nrm(ks[11], (DEPTH, D_MODEL), 0.02),
        'peer_wq': nrm(ks[12], (DEPTH, D_MODEL, PEER_HEADS * PEER_DQ), D_MODEL ** -0.5),
        'peer_subkeys': nrm(ks[13], (DEPTH, PEER_HEADS, 2, PEER_NKEYS, PEER_DQ // 2), (PEER_DQ // 2) ** -0.5),
        'peer_u': nrm(ks[14], (DEPTH, PEER_EXPERTS, D_MODEL), D_MODEL ** -0.5),
        'peer_v': nrm(ks[15], (DEPTH, PEER_EXPERTS, D_MODEL), 0.5),
        'g_final': 1.0 + nrm(ks[16], (D_MODEL,), 0.02),
    }


def reference(x, g_mix, w_in, b_gate, q_norm_g, k_norm_g, rel_bias, sink, w_branch_a, w_branch_b,
              w_out, g_ffn, peer_wq, peer_subkeys, peer_u, peer_v, g_final):
    bsz, s, d = x.shape
    rows = s // GRID_W
    row = jnp.broadcast_to(jnp.arange(rows)[:, None], (rows, GRID_W)).reshape(-1)
    col = jnp.broadcast_to(jnp.arange(GRID_W)[None, :], (rows, GRID_W)).reshape(-1)
    h = x
    for l in range(DEPTH):
        n = rmsnorm(h, g_mix[l])
        z = n @ w_in[l]
        qa, ka, va, qb, kb, vb, ga, gb = jnp.split(z, IN_OFFSETS, axis=-1)
        qa = axial_rope(rmsnorm(qa.reshape(bsz, s, A_HEADS, HEAD_DIM), q_norm_g[l]), row, col)
        ka = axial_rope(rmsnorm(ka.reshape(bsz, s, A_KV, HEAD_DIM), k_norm_g[l]), row, col)
        va = va.reshape(bsz, s, A_KV, HEAD_DIM)
        oa = dense_gqa_blocks(qa, ka, va)
        ob = windowed_gqa_sink(qb.reshape(bsz, s, B_HEADS, HEAD_DIM),
                               kb.reshape(bsz, s, B_KV, HEAD_DIM),
                               vb.reshape(bsz, s, B_KV, HEAD_DIM), rel_bias, sink[l])
        gate_a = jax.nn.sigmoid(ga + b_gate[l, :D_MODEL])
        gate_b = jax.nn.sigmoid(gb + b_gate[l, D_MODEL:])
        mix = gate_a * (oa @ w_branch_a[l]) + gate_b * (ob @ w_branch_b[l])
        h = h + mix @ w_out[l]
        h = h + peer_ffn(rmsnorm(h, g_ffn[l]), peer_wq[l], peer_subkeys[l], peer_u[l], peer_v[l])
    return rmsnorm(h, g_final)
```

```python
import functools
import math

import numpy as np
import jax
import jax.numpy as jnp
from jax import lax
from jax.experimental import pallas as pl
from jax.experimental.pallas import tpu as pltpu

F32 = jnp.float32
BF16 = jnp.bfloat16

HEAD_DIM = 64
A_HEADS, A_KV = 8, 2
B_HEADS, B_KV = 8, 2
Q_BLOCK = 128
WINDOW = 128
GRID_W = 64
ROPE_THETA = 10000.0
N_BUCKETS = 32
MAX_DISTANCE = 128
PEER_HEADS = 8
PEER_NKEYS = 128
PEER_TOPK = 16
PEER_DQ = 256
EPS = 1e-6
NEG = -1e30
INV_SQRT2 = 0.7071067811865476

V7X_LANES = 128
V7X_VMEM_LIMIT_BYTES = 56 * 1024 * 1024

_CAND = [(a, b) for a in range(PEER_TOPK) for b in range(PEER_TOPK) if (a + 1) * (b + 1) <= PEER_TOPK]
_NCAND = len(_CAND)
_NCAND_PAD = -(-_NCAND // 8) * 8


def _cparams(*sem):
    return pltpu.CompilerParams(dimension_semantics=sem, vmem_limit_bytes=V7X_VMEM_LIMIT_BYTES)


def _split_bf16(x):
    hi = x.astype(BF16)
    lo = (x - hi.astype(F32)).astype(BF16)
    return hi, lo


def _dot(a, b):
    return jnp.dot(a, b, preferred_element_type=F32)


def _dot_nt(a, b):
    return lax.dot_general(a, b, (((1,), (1,)), ((), ())), preferred_element_type=F32)


def _in_proj_kernel(x_ref, gmix_ref, w_ref, bg_ref, qg_ref, kg_ref, cos_ref, sa_ref, sb_ref, bd_ref,
                    qa_ref, ka_ref, va_ref, qb_ref, kb_ref, vb_ref, gate_ref):
    d = x_ref.shape[1]
    x = x_ref[...]
    n = x * lax.rsqrt(jnp.mean(x * x, axis=-1, keepdims=True) + EPS) * gmix_ref[...]
    nb = n.astype(BF16)
    cos, sa, sb, bd = cos_ref[...], sa_ref[...], sb_ref[...], bd_ref[...]

    def proj(a, b):
        return _dot(nb, w_ref[:, a:b])

    def norm_rope(q, gain):
        hi, lo = _split_bf16(q * q)
        ms = _dot(hi, bd) + _dot(lo, bd)
        y = q * lax.rsqrt(ms + EPS) * gain
        return y * cos + pltpu.roll(y, V7X_LANES - 16, 1) * sa + pltpu.roll(y, 16, 1) * sb

    off = 0
    qscale = HEAD_DIM ** -0.5
    for c in range(A_HEADS // 2):
        y = norm_rope(proj(off, off + 128), qg_ref[...]) * qscale
        qa_ref[2 * c] = y[:, :HEAD_DIM].astype(BF16)
        qa_ref[2 * c + 1] = y[:, HEAD_DIM:].astype(BF16)
        off += 128
    y = norm_rope(proj(off, off + 128), kg_ref[...])
    ka_ref[0] = y[:, :HEAD_DIM].astype(BF16)
    ka_ref[1] = y[:, HEAD_DIM:].astype(BF16)
    off += 128
    y = proj(off, off + 128)
    va_ref[0] = y[:, :HEAD_DIM].astype(BF16)
    va_ref[1] = y[:, HEAD_DIM:].astype(BF16)
    off += 128
    for c in range(B_HEADS // 2):
        y = proj(off, off + 128) * qscale
        qb_ref[2 * c] = y[:, :HEAD_DIM].astype(BF16)
        qb_ref[2 * c + 1] = y[:, HEAD_DIM:].astype(BF16)
        off += 128
    y = proj(off, off + 128)
    kb_ref[0] = y[:, :HEAD_DIM].astype(BF16)
    kb_ref[1] = y[:, HEAD_DIM:].astype(BF16)
    off += 128
    y = proj(off, off + 128)
    vb_ref[0] = y[:, :HEAD_DIM].astype(BF16)
    vb_ref[1] = y[:, HEAD_DIM:].astype(BF16)
    off += 128
    for c in range(2 * d // 512):
        z = proj(off + c * 512, off + (c + 1) * 512) + bg_ref[:, c * 512:(c + 1) * 512]
        gate_ref[:, c * 512:(c + 1) * 512] = (1.0 / (1.0 + jnp.exp(-z))).astype(BF16)


def _in_proj(x2, gmix, w_bf, bg, qg, kg, cos, sa, sb, bd, seq, tm):
    t, d = x2.shape
    nt = t // tm
    ns = seq // tm
    const = lambda i: (0, 0)
    tab = pl.BlockSpec((tm, 128), lambda i: (i % ns, 0))
    head = lambda nh: pl.BlockSpec((nh, tm, HEAD_DIM), lambda i: (0, i, 0))
    return pl.pallas_call(
        _in_proj_kernel,
        grid=(nt,),
        in_specs=[
            pl.BlockSpec((tm, d), lambda i: (i, 0)),
            pl.BlockSpec((1, d), const),
            pl.BlockSpec(w_bf.shape, const),
            pl.BlockSpec((1, 2 * d), const),
            pl.BlockSpec((1, 128), const),
            pl.BlockSpec((1, 128), const),
            tab, tab, tab,
            pl.BlockSpec((128, 128), const),
        ],
        out_specs=[head(A_HEADS), head(A_KV), head(A_KV), head(B_HEADS), head(B_KV), head(B_KV),
                   pl.BlockSpec((tm, 2 * d), lambda i: (i, 0))],
        out_shape=[
            jax.ShapeDtypeStruct((A_HEADS, t, HEAD_DIM), BF16),
            jax.ShapeDtypeStruct((A_KV, t, HEAD_DIM), BF16),
            jax.ShapeDtypeStruct((A_KV, t, HEAD_DIM), BF16),
            jax.ShapeDtypeStruct((B_HEADS, t, HEAD_DIM), BF16),
            jax.ShapeDtypeStruct((B_KV, t, HEAD_DIM), BF16),
            jax.ShapeDtypeStruct((B_KV, t, HEAD_DIM), BF16),
            jax.ShapeDtypeStruct((t, 2 * d), BF16),
        ],
        compiler_params=_cparams("parallel"),
        name="in_proj",
    )(x2, gmix, w_bf, bg, qg, kg, cos, sa, sb, bd)


def _attn_a_kernel(q_ref, k_ref, v_ref, o_ref):
    r, tq, hd = q_ref.shape
    q = q_ref[...].reshape(r * tq, hd)
    s = _dot_nt(q, k_ref[...])
    m = jnp.max(s, axis=-1, keepdims=True)
    p = jnp.exp(s - m)
    l = jnp.sum(p, axis=-1, keepdims=True)
    o = _dot(p.astype(BF16), v_ref[...]) / l
    for j in range(r):
        o_ref[:, j * hd:(j + 1) * hd] = o[j * tq:(j + 1) * tq].astype(BF16)


def _attn_a(qa, ka, va, bsz, seq, tq):
    t = qa.shape[1]
    r = A_HEADS // A_KV
    nq = seq // tq
    return pl.pallas_call(
        _attn_a_kernel,
        grid=(bsz, A_KV, nq),
        in_specs=[
            pl.BlockSpec((r, tq, HEAD_DIM), lambda b, g, i: (g, b * nq + i, 0)),
            pl.BlockSpec((None, seq, HEAD_DIM), lambda b, g, i: (g, b, 0)),
            pl.BlockSpec((None, seq, HEAD_DIM), lambda b, g, i: (g, b, 0)),
        ],
        out_specs=pl.BlockSpec((tq, r * HEAD_DIM), lambda b, g, i: (b * nq + i, g)),
        out_shape=jax.ShapeDtypeStruct((t, A_HEADS * HEAD_DIM), BF16),
        compiler_params=_cparams("parallel", "parallel", "parallel"),
        name="attn_a",
    )(qa, ka, va)


def _attn_b_kernel(relb_ref, sink_ref, q_ref, k_ref, v_ref, bucket_ref, o_ref, bias_ref):
    b = pl.program_id(0)
    n = pl.program_id(1)
    nblk = pl.num_programs(1)
    r = B_HEADS // B_KV
    qb = Q_BLOCK
    kw = 3 * Q_BLOCK

    @pl.when(jnp.logical_and(b == 0, n == 0))
    def _():
        bucket = bucket_ref[...]
        for h in range(B_HEADS):
            acc = jnp.zeros((qb, kw), F32)
            for k in range(N_BUCKETS):
                acc = jnp.where(bucket == k, relb_ref[k, h], acc)
            bias_ref[h] = acc

    kj = lax.broadcasted_iota(jnp.int32, (qb, kw), 1)
    kpos = n * qb - qb + kj
    valid = (bucket_ref[...] < N_BUCKETS) & (kpos >= 0) & (kpos < nblk * qb)
    start = pl.multiple_of(n * qb, qb)
    for g in range(B_KV):
        kwin = k_ref[g, pl.ds(start, kw), :]
        vwin = v_ref[g, pl.ds(start, kw), :]
        for j in range(r):
            h = g * r + j
            sc = _dot_nt(q_ref[h], kwin) + bias_ref[h]
            sc = jnp.where(valid, sc, NEG)
            sink = sink_ref[0, h]
            m = jnp.maximum(jnp.max(sc, axis=-1, keepdims=True), sink)
            p = jnp.exp(sc - m)
            den = jnp.sum(p, axis=-1, keepdims=True) + jnp.exp(sink - m)
            o = _dot(p.astype(BF16), vwin) / den
            o_ref[:, h * HEAD_DIM:(h + 1) * HEAD_DIM] = o.astype(BF16)


def _attn_b(qb, kb_pad, vb_pad, bucket, rel_bias, sink, bsz, seq):
    t = qb.shape[1]
    nb = seq // Q_BLOCK
    sp = seq + 2 * Q_BLOCK
    kv_spec = pl.BlockSpec((B_KV, None, sp, HEAD_DIM), lambda b, n: (0, b, 0, 0))
    return pl.pallas_call(
        _attn_b_kernel,
        grid=(bsz, nb),
        in_specs=[
            pl.BlockSpec(memory_space=pltpu.SMEM),
            pl.BlockSpec(memory_space=pltpu.SMEM),
            pl.BlockSpec((B_HEADS, Q_BLOCK, HEAD_DIM), lambda b, n: (0, b * nb + n, 0)),
            kv_spec, kv_spec,
            pl.BlockSpec((Q_BLOCK, 3 * Q_BLOCK), lambda b, n: (0, 0)),
        ],
        out_specs=pl.BlockSpec((Q_BLOCK, B_HEADS * HEAD_DIM), lambda b, n: (b * nb + n, 0)),
        out_shape=jax.ShapeDtypeStruct((t, B_HEADS * HEAD_DIM), BF16),
        scratch_shapes=[pltpu.VMEM((B_HEADS, Q_BLOCK, 3 * Q_BLOCK), F32)],
        compiler_params=_cparams("arbitrary", "arbitrary"),
        name="attn_b",
    )(rel_bias, sink, qb, kb_pad, vb_pad, bucket)


def _merge_kernel(oa_ref, ob_ref, gate_ref, x_ref, wa_ref, wb_ref, wo_ref, gffn_ref, wq_ref, skh_ref, skl_ref,
                  h_ref, xn_ref, sc_ref):
    d = x_ref.shape[1]
    pa = _dot(oa_ref[...], wa_ref[...])
    pb = _dot(ob_ref[...], wb_ref[...])
    mix = gate_ref[:, :d].astype(F32) * pa + gate_ref[:, d:].astype(F32) * pb
    h = x_ref[...] + _dot(mix.astype(BF16), wo_ref[...])
    h_ref[...] = h
    xn = h * lax.rsqrt(jnp.mean(h * h, axis=-1, keepdims=True) + EPS) * gffn_ref[...]
    xnb = xn.astype(BF16)
    xn_ref[...] = xnb
    half = PEER_DQ // 2
    for hc in range(2 * PEER_HEADS):
        q = _dot(xnb, wq_ref[:, hc * half:(hc + 1) * half])
        qh, ql = _split_bf16(q)
        sc_ref[hc] = _dot_nt(skh_ref[hc], qh) + _dot_nt(skh_ref[hc], ql) + _dot_nt(skl_ref[hc], qh)


def _merge(oa, ob, gate, x2, wa, wb, wo, gffn, wq, skh, skl, tm):
    t, d = x2.shape
    const2 = lambda i: (0, 0)
    const3 = lambda i: (0, 0, 0)
    row = lambda w: pl.BlockSpec((tm, w), lambda i: (i, 0))
    nsc = 2 * PEER_HEADS
    return pl.pallas_call(
        _merge_kernel,
        grid=(t // tm,),
        in_specs=[
            row(oa.shape[1]), row(ob.shape[1]), row(2 * d), row(d),
            pl.BlockSpec(wa.shape, const2), pl.BlockSpec(wb.shape, const2), pl.BlockSpec(wo.shape, const2),
            pl.BlockSpec((1, d), const2), pl.BlockSpec(wq.shape, const2),
            pl.BlockSpec(skh.shape, const3), pl.BlockSpec(skl.shape, const3),
        ],
        out_specs=[row(d), row(d), pl.BlockSpec((nsc, PEER_NKEYS, tm), lambda i: (0, 0, i))],
        out_shape=[
            jax.ShapeDtypeStruct((t, d), F32),
            jax.ShapeDtypeStruct((t, d), BF16),
            jax.ShapeDtypeStruct((nsc, PEER_NKEYS, t), F32),
        ],
        compiler_params=_cparams("parallel"),
        name="merge",
    )(oa, ob, gate, x2, wa, wb, wo, gffn, wq, skh, skl)


def _topk_ranks(s):
    nk = s.shape[0]
    rows = lax.broadcasted_iota(jnp.int32, s.shape, 0).astype(F32)
    rank = jnp.full(s.shape, float(PEER_TOPK), F32)
    x = s
    vals = []
    for a in range(PEER_TOPK):
        m = jnp.max(x, axis=0, keepdims=True)
        first = jnp.min(jnp.where(x == m, rows, float(nk)), axis=0, keepdims=True)
        hit = rows == first
        rank = jnp.where(hit, float(a), rank)
        x = jnp.where(hit, -jnp.inf, x)
        vals.append(m)
    return vals, rank


def _select_kernel(sc_ref, flat_ref, arow_ref, c1_ref, lim_ref, p2_ref, r2_ref, cand_ref):
    tb = sc_ref.shape[2]
    flat = flat_ref[...]
    arow = arow_ref[...]
    cand_ref[...] = jnp.full(cand_ref.shape, -jnp.inf, F32)

    def head_body(h, carry):
        s1 = sc_ref[2 * h]
        s2 = sc_ref[2 * h + 1]
        v1, rank1 = _topk_ranks(s1)
        v2, rank2 = _topk_ranks(s2)
        for i, (a, b) in enumerate(_CAND):
            cand_ref[i:i + 1, :] = v1[a] + v2[b]
        top = v1[0] + v2[0]
        sel = jnp.zeros(cand_ref.shape, F32)
        z = jnp.zeros((1, tb), F32)
        x = cand_ref[...]
        for _ in range(PEER_TOPK):
            m = jnp.max(x, axis=0, keepdims=True)
            first = jnp.min(jnp.where(x == m, flat, 1024.0), axis=0, keepdims=True)
            hit = flat == first
            sel = jnp.where(hit, 1.0, sel)
            x = jnp.where(hit, -jnp.inf, x)
            z = z + jnp.exp(m - top)
        lim = jnp.zeros(s1.shape, F32)
        for a in range(PEER_TOPK):
            bmax = jnp.sum(jnp.where(arow == float(a), sel, 0.0), axis=0, keepdims=True)
            lim = jnp.where(rank1 == float(a), bmax, lim)
        c1_ref[h] = jnp.exp(s1 - v1[0]) / z
        lim_ref[h] = lim
        p2_ref[h] = jnp.exp(s2 - v2[0])
        r2_ref[h] = rank2
        return carry

    lax.fori_loop(0, PEER_HEADS, head_body, 0)


def _select(sc, tb):
    nsc, nk, t = sc.shape
    flat = np.full((_NCAND_PAD, 1), 1023.0, np.float32)
    arow = np.full((_NCAND_PAD, 1), -1.0, np.float32)
    for i, (a, b) in enumerate(_CAND):
        flat[i, 0] = a * PEER_TOPK + b
        arow[i, 0] = a
    flat = jnp.asarray(np.broadcast_to(flat, (_NCAND_PAD, tb)))
    arow = jnp.asarray(np.broadcast_to(arow, (_NCAND_PAD, tb)))
    out = jax.ShapeDtypeStruct((PEER_HEADS, nk, t), F32)
    ospec = pl.BlockSpec((PEER_HEADS, nk, tb), lambda i: (0, 0, i))
    cspec = pl.BlockSpec((_NCAND_PAD, tb), lambda i: (0, 0))
    return pl.pallas_call(
        _select_kernel,
        grid=(t // tb,),
        in_specs=[pl.BlockSpec((nsc, nk, tb), lambda i: (0, 0, i)), cspec, cspec],
        out_specs=[ospec, ospec, ospec, ospec],
        out_shape=[out, out, out, out],
        scratch_shapes=[pltpu.VMEM((_NCAND_PAD, tb), F32)],
        compiler_params=_cparams("parallel"),
        name="select",
    )(sc, flat, arow)


def _experts_kernel(xn_ref, u_ref, vt_ref, c1_ref, lim_ref, p2_ref, r2_ref, h_ref, gfin_ref, o_ref,
                    acc_ref, a_ref, g_ref):
    e = pl.program_id(1)
    ne = pl.num_programs(1)
    rows = c1_ref.shape[1]
    nk = PEER_NKEYS

    @pl.when(e == 0)
    def _():
        acc_ref[...] = jnp.zeros_like(acc_ref)

    a_ref[...] = _dot_nt(u_ref[...], xn_ref[...])

    def row_body(r, carry):
        w = jnp.zeros((nk, a_ref.shape[1]), F32)
        for h in range(PEER_HEADS):
            lim = lim_ref[h, pl.ds(r, 1), :]
            c = c1_ref[h, pl.ds(r, 1), :]
            w = w + jnp.where(r2_ref[h] < lim, p2_ref[h], 0.0) * c
        off = pl.multiple_of(r * nk, nk)
        a = a_ref[pl.ds(off, nk), :]
        g = 0.5 * a * (1.0 + lax.erf(a * INV_SQRT2)) * w
        g_ref[pl.ds(off, nk), :] = g.astype(BF16)
        return carry

    lax.fori_loop(0, rows, row_body, 0)
    acc_ref[...] += _dot(vt_ref[...], g_ref[...])

    @pl.when(e == ne - 1)
    def _():
        hh = h_ref[...] + acc_ref[...].T
        o_ref[...] = hh * lax.rsqrt(jnp.mean(hh * hh, axis=-1, keepdims=True) + EPS) * gfin_ref[...]


def _experts(xn, u_bf, vt_bf, c1, lim, p2, r2, h, gfin, tt, rows):
    t, d = h.shape
    ne = u_bf.shape[0]
    et = rows * PEER_NKEYS
    tok3 = pl.BlockSpec((PEER_HEADS, PEER_NKEYS, tt), lambda i, e: (0, 0, i))
    row3 = pl.BlockSpec((PEER_HEADS, rows, tt), lambda i, e: (0, e, i))
    return pl.pallas_call(
        _experts_kernel,
        grid=(t // tt, ne // et),
        in_specs=[
            pl.BlockSpec((tt, d), lambda i, e: (i, 0)),
            pl.BlockSpec((et, d), lambda i, e: (e, 0)),
            pl.BlockSpec((d, et), lambda i, e: (0, e)),
            row3, row3, tok3, tok3,
            pl.BlockSpec((tt, d), lambda i, e: (i, 0)),
            pl.BlockSpec((1, d), lambda i, e: (0, 0)),
        ],
        out_specs=pl.BlockSpec((tt, d), lambda i, e: (i, 0)),
        out_shape=jax.ShapeDtypeStruct((t, d), F32),
        scratch_shapes=[pltpu.VMEM((d, tt), F32), pltpu.VMEM((et, tt), F32), pltpu.VMEM((et, tt), BF16)],
        compiler_params=_cparams("parallel", "arbitrary"),
        name="experts",
    )(xn, u_bf, vt_bf, c1, lim, p2, r2, h, gfin)


def _rope_tables(seq):
    rows = seq // GRID_W
    row = jnp.broadcast_to(jnp.arange(rows)[:, None], (rows, GRID_W)).reshape(-1)
    col = jnp.broadcast_to(jnp.arange(GRID_W)[None, :], (rows, GRID_W)).reshape(-1)
    half = HEAD_DIM // 4
    freqs = ROPE_THETA ** (-jnp.arange(half, dtype=F32) / half)

    def cs(pos):
        ang = pos.astype(F32)[:, None] * freqs[None, :]
        return jnp.cos(ang), jnp.sin(ang)

    cr, sr = cs(row)
    cc, sc = cs(col)
    zero = jnp.zeros_like(sr)
    cos = jnp.concatenate([cr, cr, cc, cc], axis=1)
    sa = jnp.concatenate([-sr, zero, -sc, zero], axis=1)
    sb = jnp.concatenate([zero, sr, zero, sc], axis=1)
    tile = lambda a: jnp.concatenate([a, a], axis=1)
    return tile(cos), tile(sa), tile(sb)


def _bucket_table():
    qi = jnp.arange(Q_BLOCK)
    kj = jnp.arange(3 * Q_BLOCK)
    rel = kj[None, :] - Q_BLOCK - qi[:, None]
    nbk = N_BUCKETS // 2
    max_exact = nbk // 2
    ret = jnp.where(rel > 0, nbk, 0)
    n = jnp.abs(rel)
    large = max_exact + (jnp.log(jnp.maximum(n, 1).astype(F32) / max_exact)
                         / math.log(MAX_DISTANCE / max_exact) * (nbk - max_exact)).astype(jnp.int32)
    large = jnp.minimum(large, nbk - 1)
    bucket = ret + jnp.where(n < max_exact, n, large)
    return jnp.where(jnp.abs(rel) <= WINDOW, bucket, N_BUCKETS).astype(jnp.int32)


def _tile_sizes(seq, tokens):
    tm = min(512, seq)
    tq = min(128, seq)
    tb = min(128, tokens)
    tt = min(512, tokens)
    return tm, tq, tb, tt


def kernel(x, g_mix, w_in, b_gate, q_norm_g, k_norm_g, rel_bias, sink, w_branch_a, w_branch_b, w_out, g_ffn,
           peer_wq, peer_subkeys, peer_u, peer_v, g_final):
    bsz, seq, d = x.shape
    t = bsz * seq
    depth = w_in.shape[0]
    tm, tq, tb, tt = _tile_sizes(seq, t)
    assert depth == 1, "the experts stage fuses the final norm, so only a single layer is supported"
    assert seq % tm == 0 and seq % Q_BLOCK == 0 and t % tb == 0 and t % tt == 0

    cos, sa, sb = _rope_tables(seq)
    bucket = _bucket_table()
    bd = jnp.asarray(np.kron(np.eye(2, dtype=np.float32),
                             np.full((HEAD_DIM, HEAD_DIM), 1.0 / HEAD_DIM, np.float32))).astype(BF16)
    expert_rows = 8

    h = x.reshape(t, d)
    for l in range(depth):
        qg = jnp.tile(q_norm_g[l], 2)[None, :]
        kg = jnp.tile(k_norm_g[l], 2)[None, :]
        qa, ka, va, qb, kb, vb, gate = _in_proj(
            h, g_mix[l][None, :], w_in[l].astype(BF16), b_gate[l][None, :], qg, kg, cos, sa, sb, bd, seq, tm)
        oa = _attn_a(qa, ka, va, bsz, seq, tq)
        pad = lambda a: jnp.pad(a.reshape(B_KV, bsz, seq, HEAD_DIM), ((0, 0), (0, 0), (Q_BLOCK, Q_BLOCK), (0, 0)))
        ob = _attn_b(qb, pad(kb), pad(vb), bucket, rel_bias, sink[l][None, :], bsz, seq)
        skh, skl = _split_bf16(peer_subkeys[l].reshape(2 * PEER_HEADS, PEER_NKEYS, PEER_DQ // 2))
        h1, xn, sc = _merge(oa, ob, gate, h, w_branch_a[l].astype(BF16), w_branch_b[l].astype(BF16),
                            w_out[l].astype(BF16), g_ffn[l][None, :], peer_wq[l].astype(BF16), skh, skl, tm)
        c1, lim, p2, r2 = _select(sc, tb)
        out = _experts(xn, peer_u[l].astype(BF16), peer_v[l].astype(BF16).T, c1, lim, p2, r2, h1,
                       g_final[None, :], tt, expert_rows)
    return out.reshape(bsz, seq, d)
```

```python
import functools
import math

import numpy as np
import jax
import jax.numpy as jnp
from jax import lax
from jax.experimental import pallas as pl
from jax.experimental.pallas import tpu as pltpu

F32 = jnp.float32
BF16 = jnp.bfloat16

HEAD_DIM = 64
A_HEADS, A_KV = 8, 2
B_HEADS, B_KV = 8, 2
Q_BLOCK = 128
WINDOW = 128
GRID_W = 64
ROPE_THETA = 10000.0
N_BUCKETS = 32
MAX_DISTANCE = 128
PEER_HEADS = 8
PEER_NKEYS = 128
PEER_TOPK = 16
PEER_DQ = 256
EPS = 1e-6
NEG = -1e30
INV_SQRT2 = 0.7071067811865476
LOG2E = 1.4426950408889634

V7X_LANES = 128
V7X_VMEM_LIMIT_BYTES = 56 * 1024 * 1024

_CAND = [(a, b) for a in range(PEER_TOPK) for b in range(PEER_TOPK) if (a + 1) * (b + 1) <= PEER_TOPK]
_NCAND = len(_CAND)
_NCAND_PAD = -(-_NCAND // 8) * 8


def _cparams(*sem):
    return pltpu.CompilerParams(dimension_semantics=sem, vmem_limit_bytes=V7X_VMEM_LIMIT_BYTES)


def _split_bf16(x):
    hi = x.astype(BF16)
    lo = (x - hi.astype(F32)).astype(BF16)
    return hi, lo


def _dot(a, b):
    return jnp.dot(a, b, preferred_element_type=F32)


def _dot_nt(a, b):
    return lax.dot_general(a, b, (((1,), (1,)), ((), ())), preferred_element_type=F32)


def _in_proj_kernel(x_ref, gmix_ref, w_ref, bg_ref, qg_ref, kg_ref, cos_ref, sa_ref, sb_ref, bd_ref,
                    qa_ref, ka_ref, va_ref, qb_ref, kb_ref, vb_ref, gate_ref):
    d = x_ref.shape[1]
    x = x_ref[...]
    n = x * lax.rsqrt(jnp.mean(x * x, axis=-1, keepdims=True) + EPS) * gmix_ref[...]
    nb = n.astype(BF16)
    cos, sa, sb, bd = cos_ref[...], sa_ref[...], sb_ref[...], bd_ref[...]

    def proj(a, b):
        return _dot(nb, w_ref[:, a:b])

    def norm_rope(q, gain):
        hi, lo = _split_bf16(q * q)
        ms = _dot(hi, bd) + _dot(lo, bd)
        y = q * lax.rsqrt(ms + EPS) * gain
        return y * cos + pltpu.roll(y, V7X_LANES - 16, 1) * sa + pltpu.roll(y, 16, 1) * sb

    off = 0
    qscale = HEAD_DIM ** -0.5
    for c in range(A_HEADS // 2):
        y = norm_rope(proj(off, off + 128), qg_ref[...]) * (qscale * LOG2E)
        qa_ref[2 * c] = y[:, :HEAD_DIM].astype(BF16)
        qa_ref[2 * c + 1] = y[:, HEAD_DIM:].astype(BF16)
        off += 128
    y = norm_rope(proj(off, off + 128), kg_ref[...])
    ka_ref[0] = y[:, :HEAD_DIM].astype(BF16)
    ka_ref[1] = y[:, HEAD_DIM:].astype(BF16)
    off += 128
    y = proj(off, off + 128)
    lane = lax.broadcasted_iota(jnp.int32, y.shape, 1)
    tail = jnp.where(lane == HEAD_DIM, 1.0, 0.0)
    va_ref[0] = jnp.where(lane < HEAD_DIM, y, tail).astype(BF16)
    va_ref[1] = jnp.where(lane < HEAD_DIM, pltpu.roll(y, HEAD_DIM, 1), tail).astype(BF16)
    off += 128
    for c in range(B_HEADS // 2):
        y = proj(off, off + 128) * qscale
        qb_ref[2 * c] = y[:, :HEAD_DIM].astype(BF16)
        qb_ref[2 * c + 1] = y[:, HEAD_DIM:].astype(BF16)
        off += 128
    y = proj(off, off + 128)
    kb_ref[0] = y[:, :HEAD_DIM].astype(BF16)
    kb_ref[1] = y[:, HEAD_DIM:].astype(BF16)
    off += 128
    y = proj(off, off + 128)
    vb_ref[0] = y[:, :HEAD_DIM].astype(BF16)
    vb_ref[1] = y[:, HEAD_DIM:].astype(BF16)
    off += 128
    for c in range(2 * d // 512):
        z = proj(off + c * 512, off + (c + 1) * 512) + bg_ref[:, c * 512:(c + 1) * 512]
        gate_ref[:, c * 512:(c + 1) * 512] = (1.0 / (1.0 + jnp.exp(-z))).astype(BF16)


def _in_proj(x2, gmix, w_bf, bg, qg, kg, cos, sa, sb, bd, seq, tm):
    t, d = x2.shape
    nt = t // tm
    ns = seq // tm
    const = lambda i: (0, 0)
    tab = pl.BlockSpec((tm, 128), lambda i: (i % ns, 0))
    head = lambda nh: pl.BlockSpec((nh, tm, HEAD_DIM), lambda i: (0, i, 0))
    return pl.pallas_call(
        _in_proj_kernel,
        grid=(nt,),
        in_specs=[
            pl.BlockSpec((tm, d), lambda i: (i, 0)),
            pl.BlockSpec((1, d), const),
            pl.BlockSpec(w_bf.shape, const),
            pl.BlockSpec((1, 2 * d), const),
            pl.BlockSpec((1, 128), const),
            pl.BlockSpec((1, 128), const),
            tab, tab, tab,
            pl.BlockSpec((128, 128), const),
        ],
        out_specs=[head(A_HEADS), head(A_KV), pl.BlockSpec((A_KV, tm, 2 * HEAD_DIM), lambda i: (0, i, 0)),
                   head(B_HEADS), head(B_KV), head(B_KV),
                   pl.BlockSpec((tm, 2 * d), lambda i: (i, 0))],
        out_shape=[
            jax.ShapeDtypeStruct((A_HEADS, t, HEAD_DIM), BF16),
            jax.ShapeDtypeStruct((A_KV, t, HEAD_DIM), BF16),
            jax.ShapeDtypeStruct((A_KV, t, 2 * HEAD_DIM), BF16),
            jax.ShapeDtypeStruct((B_HEADS, t, HEAD_DIM), BF16),
            jax.ShapeDtypeStruct((B_KV, t, HEAD_DIM), BF16),
            jax.ShapeDtypeStruct((B_KV, t, HEAD_DIM), BF16),
            jax.ShapeDtypeStruct((t, 2 * d), BF16),
        ],
        compiler_params=_cparams("parallel"),
        name="in_proj",
    )(x2, gmix, w_bf, bg, qg, kg, cos, sa, sb, bd)


def _attn_a_kernel(q_ref, k_ref, v_ref, o_ref, *, kc):
    r, tq, hd = q_ref.shape
    seq = k_ref.shape[0]
    q = q_ref[...].reshape(r * tq, hd)
    m = jnp.full((r * tq, 1), -jnp.inf, F32)
    acc = jnp.zeros((r * tq, v_ref.shape[1]), F32)
    for c in range(seq // kc):
        s = _dot_nt(q, k_ref[c * kc:(c + 1) * kc, :])
        m_new = jnp.maximum(m, jnp.max(s, axis=-1, keepdims=True))
        p = jnp.exp2(s - m_new)
        acc = jnp.exp2(m - m_new) * acc + _dot(p.astype(BF16), v_ref[c * kc:(c + 1) * kc, :])
        m = m_new
    o = acc[:, :hd] / acc[:, hd:hd + 1]
    for j in range(r):
        o_ref[:, j * hd:(j + 1) * hd] = o[j * tq:(j + 1) * tq].astype(BF16)


def _attn_a(qa, ka, va, bsz, seq, tq):
    t = qa.shape[1]
    r = A_HEADS // A_KV
    nq = seq // tq
    return pl.pallas_call(
        functools.partial(_attn_a_kernel, kc=min(512, seq)),
        grid=(bsz, A_KV, nq),
        in_specs=[
            pl.BlockSpec((r, tq, HEAD_DIM), lambda b, g, i: (g, b * nq + i, 0)),
            pl.BlockSpec((None, seq, HEAD_DIM), lambda b, g, i: (g, b, 0)),
            pl.BlockSpec((None, seq, 2 * HEAD_DIM), lambda b, g, i: (g, b, 0)),
        ],
        out_specs=pl.BlockSpec((tq, r * HEAD_DIM), lambda b, g, i: (b * nq + i, g)),
        out_shape=jax.ShapeDtypeStruct((t, A_HEADS * HEAD_DIM), BF16),
        compiler_params=_cparams("parallel", "parallel", "parallel"),
        name="attn_a",
    )(qa, ka, va)


def _attn_b_kernel(relb_ref, sink_ref, q_ref, k_ref, v_ref, bucket_ref, o_ref, bias_ref):
    b = pl.program_id(0)
    n = pl.program_id(1)
    nblk = pl.num_programs(1)
    r = B_HEADS // B_KV
    qb = Q_BLOCK
    kw = 3 * Q_BLOCK

    @pl.when(jnp.logical_and(b == 0, n == 0))
    def _():
        bucket = bucket_ref[...]
        for h in range(B_HEADS):
            acc = jnp.zeros((qb, kw), F32)
            for k in range(N_BUCKETS):
                acc = jnp.where(bucket == k, relb_ref[k, h], acc)
            bias_ref[h] = acc

    kj = lax.broadcasted_iota(jnp.int32, (qb, kw), 1)
    kpos = n * qb - qb + kj
    valid = (bucket_ref[...] < N_BUCKETS) & (kpos >= 0) & (kpos < nblk * qb)
    start = pl.multiple_of(n * qb, qb)
    for g in range(B_KV):
        kwin = k_ref[g, pl.ds(start, kw), :]
        vwin = v_ref[g, pl.ds(start, kw), :]
        for j in range(r):
            h = g * r + j
            sc = _dot_nt(q_ref[h], kwin) + bias_ref[h]
            sc = jnp.where(valid, sc, NEG)
            sink = sink_ref[0, h]
            m = jnp.maximum(jnp.max(sc, axis=-1, keepdims=True), sink)
            p = jnp.exp(sc - m)
            den = jnp.sum(p, axis=-1, keepdims=True) + jnp.exp(sink - m)
            o = _dot(p.astype(BF16), vwin) / den
            o_ref[:, h * HEAD_DIM:(h + 1) * HEAD_DIM] = o.astype(BF16)


def _attn_b(qb, kb_pad, vb_pad, bucket, rel_bias, sink, bsz, seq):
    t = qb.shape[1]
    nb = seq // Q_BLOCK
    sp = seq + 2 * Q_BLOCK
    kv_spec = pl.BlockSpec((B_KV, None, sp, HEAD_DIM), lambda b, n: (0, b, 0, 0))
    return pl.pallas_call(
        _attn_b_kernel,
        grid=(bsz, nb),
        in_specs=[
            pl.BlockSpec(memory_space=pltpu.SMEM),
            pl.BlockSpec(memory_space=pltpu.SMEM),
            pl.BlockSpec((B_HEADS, Q_BLOCK, HEAD_DIM), lambda b, n: (0, b * nb + n, 0)),
            kv_spec, kv_spec,
            pl.BlockSpec((Q_BLOCK, 3 * Q_BLOCK), lambda b, n: (0, 0)),
        ],
        out_specs=pl.BlockSpec((Q_BLOCK, B_HEADS * HEAD_DIM), lambda b, n: (b * nb + n, 0)),
        out_shape=jax.ShapeDtypeStruct((t, B_HEADS * HEAD_DIM), BF16),
        scratch_shapes=[pltpu.VMEM((B_HEADS, Q_BLOCK, 3 * Q_BLOCK), F32)],
        compiler_params=_cparams("arbitrary", "arbitrary"),
        name="attn_b",
    )(rel_bias, sink, qb, kb_pad, vb_pad, bucket)


def _merge_kernel(oa_ref, ob_ref, gate_ref, x_ref, wa_ref, wb_ref, wo_ref, gffn_ref, wq_ref, skh_ref, skl_ref,
                  h_ref, xn_ref, sc_ref):
    d = x_ref.shape[1]
    pa = _dot(oa_ref[...], wa_ref[...])
    pb = _dot(ob_ref[...], wb_ref[...])
    mix = gate_ref[:, :d].astype(F32) * pa + gate_ref[:, d:].astype(F32) * pb
    h = x_ref[...] + _dot(mix.astype(BF16), wo_ref[...])
    h_ref[...] = h
    xn = h * lax.rsqrt(jnp.mean(h * h, axis=-1, keepdims=True) + EPS) * gffn_ref[...]
    xnb = xn.astype(BF16)
    xn_ref[...] = xnb
    half = PEER_DQ // 2
    for hc in range(2 * PEER_HEADS):
        q = _dot(xnb, wq_ref[:, hc * half:(hc + 1) * half])
        qh, ql = _split_bf16(q)
        sc_ref[hc] = _dot_nt(skh_ref[hc], qh) + _dot_nt(skh_ref[hc], ql) + _dot_nt(skl_ref[hc], qh)


def _merge(oa, ob, gate, x2, wa, wb, wo, gffn, wq, skh, skl, tm):
    t, d = x2.shape
    const2 = lambda i: (0, 0)
    const3 = lambda i: (0, 0, 0)
    row = lambda w: pl.BlockSpec((tm, w), lambda i: (i, 0))
    nsc = 2 * PEER_HEADS
    return pl.pallas_call(
        _merge_kernel,
        grid=(t // tm,),
        in_specs=[
            row(oa.shape[1]), row(ob.shape[1]), row(2 * d), row(d),
            pl.BlockSpec(wa.shape, const2), pl.BlockSpec(wb.shape, const2), pl.BlockSpec(wo.shape, const2),
            pl.BlockSpec((1, d), const2), pl.BlockSpec(wq.shape, const2),
            pl.BlockSpec(skh.shape, const3), pl.BlockSpec(skl.shape, const3),
        ],
        out_specs=[row(d), row(d), pl.BlockSpec((nsc, PEER_NKEYS, tm), lambda i: (0, 0, i))],
        out_shape=[
            jax.ShapeDtypeStruct((t, d), F32),
            jax.ShapeDtypeStruct((t, d), BF16),
            jax.ShapeDtypeStruct((nsc, PEER_NKEYS, t), F32),
        ],
        compiler_params=_cparams("parallel"),
        name="merge",
    )(oa, ob, gate, x2, wa, wb, wo, gffn, wq, skh, skl)


def _topk_ranks(s):
    nk = s.shape[0]
    rows = lax.broadcasted_iota(jnp.int32, s.shape, 0).astype(F32)
    rank = jnp.full(s.shape, float(PEER_TOPK), F32)
    x = s
    vals = []
    for a in range(PEER_TOPK):
        m = jnp.max(x, axis=0, keepdims=True)
        first = jnp.min(jnp.where(x == m, rows, float(nk)), axis=0, keepdims=True)
        hit = rows == first
        rank = jnp.where(hit, float(a), rank)
        x = jnp.where(hit, -jnp.inf, x)
        vals.append(m)
    return vals, rank


def _bf16_pair_bits(x):
    hi = pltpu.bitcast(x.astype(BF16).astype(F32), jnp.int32)
    return hi | lax.shift_right_logical(hi, 16)


def _select_kernel(sc_ref, flat_ref, arow_ref, c1_ref, lim_ref, p2_ref, r2_ref, cand_ref):
    tb = sc_ref.shape[2]
    flat = flat_ref[...]
    arow = arow_ref[...]
    cand_ref[...] = jnp.full(cand_ref.shape, -jnp.inf, F32)

    def head_body(h, carry):
        s1 = sc_ref[2 * h]
        s2 = sc_ref[2 * h + 1]
        v1, rank1 = _topk_ranks(s1)
        v2, rank2 = _topk_ranks(s2)
        for i, (a, b) in enumerate(_CAND):
            cand_ref[i:i + 1, :] = v1[a] + v2[b]
        top = v1[0] + v2[0]
        sel = jnp.zeros(cand_ref.shape, F32)
        z = jnp.zeros((1, tb), F32)
        x = cand_ref[...]
        for _ in range(PEER_TOPK):
            m = jnp.max(x, axis=0, keepdims=True)
            first = jnp.min(jnp.where(x == m, flat, 1024.0), axis=0, keepdims=True)
            hit = flat == first
            sel = jnp.where(hit, 1.0, sel)
            x = jnp.where(hit, -jnp.inf, x)
            z = z + jnp.exp(m - top)
        lim = jnp.zeros(s1.shape, F32)
        for a in range(PEER_TOPK):
            bmax = jnp.sum(jnp.where(arow == float(a), sel, 0.0), axis=0, keepdims=True)
            lim = jnp.where(rank1 == float(a), bmax, lim)
        c1_ref[h] = _bf16_pair_bits(0.5 * jnp.exp(s1 - v1[0]) / z)
        lim_ref[h] = _bf16_pair_bits(lim)
        p2_ref[h] = jnp.exp(s2 - v2[0]).astype(BF16)
        r2_ref[h] = rank2.astype(BF16)
        return carry

    lax.fori_loop(0, PEER_HEADS, head_body, 0)


def _select(sc, tb):
    nsc, nk, t = sc.shape
    flat = np.full((_NCAND_PAD, 1), 1023.0, np.float32)
    arow = np.full((_NCAND_PAD, 1), -1.0, np.float32)
    for i, (a, b) in enumerate(_CAND):
        flat[i, 0] = a * PEER_TOPK + b
        arow[i, 0] = a
    flat = jnp.asarray(np.broadcast_to(flat, (_NCAND_PAD, tb)))
    arow = jnp.asarray(np.broadcast_to(arow, (_NCAND_PAD, tb)))
    out = lambda dt: jax.ShapeDtypeStruct((PEER_HEADS, nk, t), dt)
    ospec = pl.BlockSpec((PEER_HEADS, nk, tb), lambda i: (0, 0, i))
    cspec = pl.BlockSpec((_NCAND_PAD, tb), lambda i: (0, 0))
    return pl.pallas_call(
        _select_kernel,
        grid=(t // tb,),
        in_specs=[pl.BlockSpec((nsc, nk, tb), lambda i: (0, 0, i)), cspec, cspec],
        out_specs=[ospec, ospec, ospec, ospec],
        out_shape=[out(jnp.int32), out(jnp.int32), out(BF16), out(BF16)],
        scratch_shapes=[pltpu.VMEM((_NCAND_PAD, tb), F32)],
        compiler_params=_cparams("parallel"),
        name="select",
    )(sc, flat, arow)


def _experts_kernel(xn_ref, u_ref, vt_ref, c1_ref, lim_ref, p2_ref, r2_ref, h_ref, gfin_ref, o_ref,
                    acc_ref, g_ref):
    e = pl.program_id(1)
    nblk = pl.num_programs(1) - 1
    rows = c1_ref.shape[1]
    nk = PEER_NKEYS

    slot = lax.rem(e, 2)
    g_new = g_ref.at[slot]
    g_old = g_ref.at[1 - slot]

    @pl.when(e == 0)
    def _():
        acc_ref[...] = jnp.zeros_like(acc_ref)
        g_old[...] = jnp.zeros_like(g_old)

    def row_bf16(ref, h, r):
        x = jnp.broadcast_to(ref[h, r:r + 1, :], (8, ref.shape[2]))
        return jnp.concatenate([pltpu.bitcast(x, BF16)] * (nk // 16), axis=0)

    a = _dot_nt(u_ref[...], xn_ref[...])
    acc_ref[...] += _dot(vt_ref[...], g_old[...])
    for r in range(rows):
        w = None
        for h in range(PEER_HEADS):
            term = jnp.where(r2_ref[h] < row_bf16(lim_ref, h, r), p2_ref[h], jnp.zeros((), BF16)) \
                * row_bf16(c1_ref, h, r)
            w = term if w is None else w + term
        ar = a[r * nk:(r + 1) * nk]
        g = (ar * (1.0 + lax.erf(ar * INV_SQRT2))).astype(BF16) * w
        g_new[r * nk:(r + 1) * nk, :] = g

    @pl.when(e == nblk)
    def _():
        hh = h_ref[...] + acc_ref[...].T
        o_ref[...] = hh * lax.rsqrt(jnp.mean(hh * hh, axis=-1, keepdims=True) + EPS) * gfin_ref[...]


def _experts(xn, u_bf, vt_bf, c1, lim, p2, r2, h, gfin, tt, rows):
    t, d = h.shape
    et = rows * PEER_NKEYS
    nblk = u_bf.shape[0] // et
    cur = lambda e: jnp.minimum(e, nblk - 1)
    prev = lambda e: jnp.maximum(e - 1, 0)
    tok3 = pl.BlockSpec((PEER_HEADS, PEER_NKEYS, tt), lambda i, e: (0, 0, i))
    row3 = pl.BlockSpec((PEER_HEADS, rows, tt), lambda i, e: (0, cur(e), i))
    return pl.pallas_call(
        _experts_kernel,
        grid=(t // tt, nblk + 1),
        in_specs=[
            pl.BlockSpec((tt, d), lambda i, e: (i, 0)),
            pl.BlockSpec((et, d), lambda i, e: (cur(e), 0)),
            pl.BlockSpec((d, et), lambda i, e: (0, prev(e))),
            row3, row3, tok3, tok3,
            pl.BlockSpec((tt, d), lambda i, e: (i, 0)),
            pl.BlockSpec((1, d), lambda i, e: (0, 0)),
        ],
        out_specs=pl.BlockSpec((tt, d), lambda i, e: (i, 0)),
        out_shape=jax.ShapeDtypeStruct((t, d), F32),
        scratch_shapes=[pltpu.VMEM((d, tt), F32), pltpu.VMEM((2, et, tt), BF16)],
        compiler_params=_cparams("parallel", "arbitrary"),
        name="experts",
    )(xn, u_bf, vt_bf, c1, lim, p2, r2, h, gfin)


def _rope_tables(seq):
    rows = seq // GRID_W
    row = jnp.broadcast_to(jnp.arange(rows)[:, None], (rows, GRID_W)).reshape(-1)
    col = jnp.broadcast_to(jnp.arange(GRID_W)[None, :], (rows, GRID_W)).reshape(-1)
    half = HEAD_DIM // 4
    freqs = ROPE_THETA ** (-jnp.arange(half, dtype=F32) / half)

    def cs(pos):
        ang = pos.astype(F32)[:, None] * freqs[None, :]
        return jnp.cos(ang), jnp.sin(ang)

    cr, sr = cs(row)
    cc, sc = cs(col)
    zero = jnp.zeros_like(sr)
    cos = jnp.concatenate([cr, cr, cc, cc], axis=1)
    sa = jnp.concatenate([-sr, zero, -sc, zero], axis=1)
    sb = jnp.concatenate([zero, sr, zero, sc], axis=1)
    tile = lambda a: jnp.concatenate([a, a], axis=1)
    return tile(cos), tile(sa), tile(sb)


def _bucket_table():
    qi = jnp.arange(Q_BLOCK)
    kj = jnp.arange(3 * Q_BLOCK)
    rel = kj[None, :] - Q_BLOCK - qi[:, None]
    nbk = N_BUCKETS // 2
    max_exact = nbk // 2
    ret = jnp.where(rel > 0, nbk, 0)
    n = jnp.abs(rel)
    large = max_exact + (jnp.log(jnp.maximum(n, 1).astype(F32) / max_exact)
                         / math.log(MAX_DISTANCE / max_exact) * (nbk - max_exact)).astype(jnp.int32)
    large = jnp.minimum(large, nbk - 1)
    bucket = ret + jnp.where(n < max_exact, n, large)
    return jnp.where(jnp.abs(rel) <= WINDOW, bucket, N_BUCKETS).astype(jnp.int32)


def _tile_sizes(seq, tokens):
    tm = min(512, seq)
    tq = min(128, seq)
    tb = min(128, tokens)
    tt = min(512, tokens)
    return tm, tq, tb, tt


def kernel(x, g_mix, w_in, b_gate, q_norm_g, k_norm_g, rel_bias, sink, w_branch_a, w_branch_b, w_out, g_ffn,
           peer_wq, peer_subkeys, peer_u, peer_v, g_final):
    bsz, seq, d = x.shape
    t = bsz * seq
    depth = w_in.shape[0]
    tm, tq, tb, tt = _tile_sizes(seq, t)
    assert depth == 1, "the experts stage fuses the final norm, so only a single layer is supported"
    assert seq % tm == 0 and seq % Q_BLOCK == 0 and t % tb == 0 and t % tt == 0

    cos, sa, sb = _rope_tables(seq)
    bucket = _bucket_table()
    bd = jnp.asarray(np.kron(np.eye(2, dtype=np.float32),
                             np.full((HEAD_DIM, HEAD_DIM), 1.0 / HEAD_DIM, np.float32))).astype(BF16)
    expert_rows = 8

    h = x.reshape(t, d)
    for l in range(depth):
        qg = jnp.tile(q_norm_g[l], 2)[None, :]
        kg = jnp.tile(k_norm_g[l], 2)[None, :]
        qa, ka, va, qb, kb, vb, gate = _in_proj(
            h, g_mix[l][None, :], w_in[l].astype(BF16), b_gate[l][None, :], qg, kg, cos, sa, sb, bd, seq, tm)
        oa = _attn_a(qa, ka, va, bsz, seq, tq)
        pad = lambda a: jnp.pad(a.reshape(B_KV, bsz, seq, HEAD_DIM), ((0, 0), (0, 0), (Q_BLOCK, Q_BLOCK), (0, 0)))
        ob = _attn_b(qb, pad(kb), pad(vb), bucket, rel_bias, sink[l][None, :], bsz, seq)
        skh, skl = _split_bf16(peer_subkeys[l].reshape(2 * PEER_HEADS, PEER_NKEYS, PEER_DQ // 2))
        h1, xn, sc = _merge(oa, ob, gate, h, w_branch_a[l].astype(BF16), w_branch_b[l].astype(BF16),
                            w_out[l].astype(BF16), g_ffn[l][None, :], peer_wq[l].astype(BF16), skh, skl, tm)
        c1, lim, p2, r2 = _select(sc, tb)
        out = _experts(xn, peer_u[l].astype(BF16), peer_v[l].astype(BF16).T, c1, lim, p2, r2, h1,
                       g_final[None, :], tt, expert_rows)
    return out.reshape(bsz, seq, d)
```

```python
import functools
import math

import numpy as np
import jax
import jax.numpy as jnp
from jax import lax
from jax.experimental import pallas as pl
from jax.experimental.pallas import tpu as pltpu

F32 = jnp.float32
BF16 = jnp.bfloat16

HEAD_DIM = 64
A_HEADS, A_KV = 8, 2
B_HEADS, B_KV = 8, 2
Q_BLOCK = 128
WINDOW = 128
GRID_W = 64
ROPE_THETA = 10000.0
N_BUCKETS = 32
MAX_DISTANCE = 128
PEER_HEADS = 8
PEER_NKEYS = 128
PEER_TOPK = 16
PEER_DQ = 256
EPS = 1e-6
NEG = -1e30
INV_SQRT2 = 0.7071067811865476
LOG2E = 1.4426950408889634

V7X_LANES = 128
V7X_VMEM_LIMIT_BYTES = 56 * 1024 * 1024

_CAND = [(a, b) for a in range(PEER_TOPK) for b in range(PEER_TOPK) if (a + 1) * (b + 1) <= PEER_TOPK]
_NCAND = len(_CAND)
_NCAND_PAD = -(-_NCAND // 8) * 8


def _cparams(*sem):
    return pltpu.CompilerParams(dimension_semantics=sem, vmem_limit_bytes=V7X_VMEM_LIMIT_BYTES)


def _split_bf16(x):
    hi = x.astype(BF16)
    lo = (x - hi.astype(F32)).astype(BF16)
    return hi, lo


def _dot(a, b):
    return jnp.dot(a, b, preferred_element_type=F32)


def _dot_nt(a, b):
    return lax.dot_general(a, b, (((1,), (1,)), ((), ())), preferred_element_type=F32)


def _in_proj_kernel(x_ref, gmix_ref, w_ref, bg_ref, qg_ref, kg_ref, cos_ref, sa_ref, sb_ref, bd_ref,
                    qa_ref, ka_ref, va_ref, qb_ref, kb_ref, vb_ref, gate_ref):
    d = x_ref.shape[1]
    x = x_ref[...]
    n = x * lax.rsqrt(jnp.mean(x * x, axis=-1, keepdims=True) + EPS) * gmix_ref[...]
    nb = n.astype(BF16)
    cos, sa, sb, bd = cos_ref[...], sa_ref[...], sb_ref[...], bd_ref[...]

    def norm_rope(q, gain):
        hi, lo = _split_bf16(q * q)
        ms = _dot(hi, bd) + _dot(lo, bd)
        y = q * lax.rsqrt(ms + EPS) * gain
        return y * cos + pltpu.roll(y, V7X_LANES - 16, 1) * sa + pltpu.roll(y, 16, 1) * sb

    def put_heads(ref, first, y):
        ref[first] = y[:, :HEAD_DIM].astype(BF16)
        ref[first + 1] = y[:, HEAD_DIM:].astype(BF16)

    def put_values(ref, y):
        lane = lax.broadcasted_iota(jnp.int32, y.shape, 1)
        tail = jnp.where(lane == HEAD_DIM, 1.0, 0.0)
        ref[0] = jnp.where(lane < HEAD_DIM, y, tail).astype(BF16)
        ref[1] = jnp.where(lane < HEAD_DIM, pltpu.roll(y, HEAD_DIM, 1), tail).astype(BF16)

    qscale = HEAD_DIM ** -0.5 * LOG2E
    n_attn = (A_HEADS + 2 * A_KV + B_HEADS + 2 * B_KV) * HEAD_DIM
    z = _dot(nb, w_ref[:, :n_attn])
    col = lambda c: z[:, c * 128:(c + 1) * 128]
    c = 0
    for j in range(A_HEADS // 2):
        put_heads(qa_ref, 2 * j, norm_rope(col(c), qg_ref[...]) * qscale)
        c += 1
    put_heads(ka_ref, 0, norm_rope(col(c), kg_ref[...]))
    put_values(va_ref, col(c + 1))
    c += 2
    for j in range(B_HEADS // 2):
        put_heads(qb_ref, 2 * j, col(c) * qscale)
        c += 1
    put_heads(kb_ref, 0, col(c))
    put_values(vb_ref, col(c + 1))
    for j in range(2 * d // 512):
        lo, hi = n_attn + j * 512, n_attn + (j + 1) * 512
        zg = _dot(nb, w_ref[:, lo:hi]) + bg_ref[:, j * 512:(j + 1) * 512]
        gate_ref[:, j * 512:(j + 1) * 512] = (1.0 / (1.0 + jnp.exp(-zg))).astype(BF16)


def _in_proj(x2, gmix, w_bf, bg, qg, kg, cos, sa, sb, bd, seq, tm):
    t, d = x2.shape
    nt = t // tm
    ns = seq // tm
    const = lambda i: (0, 0)
    tab = pl.BlockSpec((tm, 128), lambda i: (i % ns, 0))
    head = lambda nh: pl.BlockSpec((nh, tm, HEAD_DIM), lambda i: (0, i, 0))
    wide = lambda nh: pl.BlockSpec((nh, tm, 2 * HEAD_DIM), lambda i: (0, i, 0))
    return pl.pallas_call(
        _in_proj_kernel,
        grid=(nt,),
        in_specs=[
            pl.BlockSpec((tm, d), lambda i: (i, 0)),
            pl.BlockSpec((1, d), const),
            pl.BlockSpec(w_bf.shape, const),
            pl.BlockSpec((1, 2 * d), const),
            pl.BlockSpec((1, 128), const),
            pl.BlockSpec((1, 128), const),
            tab, tab, tab,
            pl.BlockSpec((128, 128), const),
        ],
        out_specs=[head(A_HEADS), head(A_KV), wide(A_KV), head(B_HEADS), head(B_KV), wide(B_KV),
                   pl.BlockSpec((tm, 2 * d), lambda i: (i, 0))],
        out_shape=[
            jax.ShapeDtypeStruct((A_HEADS, t, HEAD_DIM), BF16),
            jax.ShapeDtypeStruct((A_KV, t, HEAD_DIM), BF16),
            jax.ShapeDtypeStruct((A_KV, t, 2 * HEAD_DIM), BF16),
            jax.ShapeDtypeStruct((B_HEADS, t, HEAD_DIM), BF16),
            jax.ShapeDtypeStruct((B_KV, t, HEAD_DIM), BF16),
            jax.ShapeDtypeStruct((B_KV, t, 2 * HEAD_DIM), BF16),
            jax.ShapeDtypeStruct((t, 2 * d), BF16),
        ],
        compiler_params=_cparams("parallel"),
        name="in_proj",
    )(x2, gmix, w_bf, bg, qg, kg, cos, sa, sb, bd)


def _attn_a_kernel(q_ref, k_ref, v_ref, o_ref, *, kc):
    r, tq, hd = q_ref.shape
    seq = k_ref.shape[0]
    q = q_ref[...].reshape(r * tq, hd)
    m = jnp.full((r * tq, 1), -jnp.inf, F32)
    acc = jnp.zeros((r * tq, v_ref.shape[1]), F32)
    for c in range(seq // kc):
        s = _dot_nt(q, k_ref[c * kc:(c + 1) * kc, :])
        m_new = jnp.maximum(m, jnp.max(s, axis=-1, keepdims=True))
        p = jnp.exp2(s - m_new)
        acc = jnp.exp2(m - m_new) * acc + _dot(p.astype(BF16), v_ref[c * kc:(c + 1) * kc, :])
        m = m_new
    o = acc[:, :hd] / acc[:, hd:hd + 1]
    for j in range(r):
        o_ref[:, j * hd:(j + 1) * hd] = o[j * tq:(j + 1) * tq].astype(BF16)


def _attn_a(qa, ka, va, bsz, seq, tq):
    t = qa.shape[1]
    r = A_HEADS // A_KV
    nq = seq // tq
    return pl.pallas_call(
        functools.partial(_attn_a_kernel, kc=min(512, seq)),
        grid=(bsz, A_KV, nq),
        in_specs=[
            pl.BlockSpec((r, tq, HEAD_DIM), lambda b, g, i: (g, b * nq + i, 0)),
            pl.BlockSpec((None, seq, HEAD_DIM), lambda b, g, i: (g, b, 0)),
            pl.BlockSpec((None, seq, 2 * HEAD_DIM), lambda b, g, i: (g, b, 0)),
        ],
        out_specs=pl.BlockSpec((tq, r * HEAD_DIM), lambda b, g, i: (b * nq + i, g)),
        out_shape=jax.ShapeDtypeStruct((t, A_HEADS * HEAD_DIM), BF16),
        compiler_params=_cparams("parallel", "parallel", "parallel"),
        name="attn_a",
    )(qa, ka, va)


def _attn_b_kernel(relb_ref, sink_ref, q_ref, k_ref, v_ref, bucket_ref, o_ref, bias_ref):
    b = pl.program_id(0)
    n = pl.program_id(1)
    nblk = pl.num_programs(1)
    r = B_HEADS // B_KV
    qb = Q_BLOCK
    kw = 3 * Q_BLOCK

    @pl.when(jnp.logical_and(b == 0, n == 0))
    def _():
        bucket = bucket_ref[...]
        for h in range(B_HEADS):
            acc = jnp.full((qb, kw), NEG, F32)
            for k in range(N_BUCKETS):
                acc = jnp.where(bucket == k, relb_ref[k, h] * LOG2E, acc)
            bias_ref[h * qb:(h + 1) * qb, :] = acc

    nq = q_ref.shape[1] // qb
    kj = lax.broadcasted_iota(jnp.int32, (1, kw), 1)
    for i in range(nq):
        blk = n * nq + i
        kpos = blk * qb - qb + kj
        inside = (kpos >= 0) & (kpos < nblk * nq * qb)
        start = pl.multiple_of(blk * qb, qb)
        for g in range(B_KV):
            kwin = k_ref[g, pl.ds(start, kw), :]
            vwin = v_ref[g, pl.ds(start, kw), :]
            q = q_ref[g * r:(g + 1) * r, i * qb:(i + 1) * qb, :].reshape(r * qb, HEAD_DIM)
            sc = _dot_nt(q, kwin) + bias_ref[g * r * qb:(g + 1) * r * qb, :]
            sc = jnp.where(inside, sc, NEG)
            sink = jnp.concatenate(
                [jnp.full((qb, 1), sink_ref[0, g * r + j] * LOG2E, F32) for j in range(r)], axis=0)
            m = jnp.maximum(jnp.max(sc, axis=-1, keepdims=True), sink)
            pv = _dot(jnp.exp2(sc - m).astype(BF16), vwin)
            o = pv[:, :HEAD_DIM] / (pv[:, HEAD_DIM:HEAD_DIM + 1] + jnp.exp2(sink - m))
            for j in range(r):
                h = g * r + j
                o_ref[i * qb:(i + 1) * qb, h * HEAD_DIM:(h + 1) * HEAD_DIM] = o[j * qb:(j + 1) * qb].astype(BF16)


def _attn_b(qb, kb_pad, vb_pad, bucket, rel_bias, sink, bsz, seq):
    t = qb.shape[1]
    nb = seq // Q_BLOCK
    sp = seq + 2 * Q_BLOCK
    nq = 2 if nb % 2 == 0 else 1
    ns = nb // nq
    return pl.pallas_call(
        _attn_b_kernel,
        grid=(bsz, ns),
        in_specs=[
            pl.BlockSpec(memory_space=pltpu.SMEM),
            pl.BlockSpec(memory_space=pltpu.SMEM),
            pl.BlockSpec((B_HEADS, nq * Q_BLOCK, HEAD_DIM), lambda b, n: (0, b * ns + n, 0)),
            pl.BlockSpec((B_KV, None, sp, HEAD_DIM), lambda b, n: (0, b, 0, 0)),
            pl.BlockSpec((B_KV, None, sp, 2 * HEAD_DIM), lambda b, n: (0, b, 0, 0)),
            pl.BlockSpec((Q_BLOCK, 3 * Q_BLOCK), lambda b, n: (0, 0)),
        ],
        out_specs=pl.BlockSpec((nq * Q_BLOCK, B_HEADS * HEAD_DIM), lambda b, n: (b * ns + n, 0)),
        out_shape=jax.ShapeDtypeStruct((t, B_HEADS * HEAD_DIM), BF16),
        scratch_shapes=[pltpu.VMEM((B_HEADS * Q_BLOCK, 3 * Q_BLOCK), F32)],
        compiler_params=_cparams("arbitrary", "arbitrary"),
        name="attn_b",
    )(rel_bias, sink, qb, kb_pad, vb_pad, bucket)


def _merge_kernel(oa_ref, ob_ref, gate_ref, x_ref, wa_ref, wb_ref, wo_ref, gffn_ref, wq_ref, sk3_ref,
                  h_ref, xnt_ref, sc_ref):
    d = x_ref.shape[1]
    pa = _dot(oa_ref[...], wa_ref[...])
    pb = _dot(ob_ref[...], wb_ref[...])
    mix = gate_ref[:, :d].astype(F32) * pa + gate_ref[:, d:].astype(F32) * pb
    h = x_ref[...] + _dot(mix.astype(BF16), wo_ref[...])
    h_ref[...] = h
    xn = h * lax.rsqrt(jnp.mean(h * h, axis=-1, keepdims=True) + EPS) * gffn_ref[...]
    xnb = xn.astype(BF16)
    xnt_ref[...] = xn.T.astype(BF16)
    half = PEER_DQ // 2
    qh, ql = _split_bf16(_dot(xnb, wq_ref[...]))
    for hc in range(2 * PEER_HEADS):
        cols = slice(hc * half, (hc + 1) * half)
        q3 = jnp.concatenate([qh[:, cols], ql[:, cols], qh[:, cols]], axis=1)
        sc_ref[hc] = _dot_nt(sk3_ref[hc], q3)


def _merge(oa, ob, gate, x2, wa, wb, wo, gffn, wq, sk3, tm):
    t, d = x2.shape
    const2 = lambda i: (0, 0)
    const3 = lambda i: (0, 0, 0)
    row = lambda w: pl.BlockSpec((tm, w), lambda i: (i, 0))
    nsc = 2 * PEER_HEADS
    return pl.pallas_call(
        _merge_kernel,
        grid=(t // tm,),
        in_specs=[
            row(oa.shape[1]), row(ob.shape[1]), row(2 * d), row(d),
            pl.BlockSpec(wa.shape, const2), pl.BlockSpec(wb.shape, const2), pl.BlockSpec(wo.shape, const2),
            pl.BlockSpec((1, d), const2), pl.BlockSpec(wq.shape, const2),
            pl.BlockSpec(sk3.shape, const3),
        ],
        out_specs=[row(d), pl.BlockSpec((d, tm), lambda i: (0, i)),
                   pl.BlockSpec((nsc, PEER_NKEYS, tm), lambda i: (0, 0, i))],
        out_shape=[
            jax.ShapeDtypeStruct((t, d), F32),
            jax.ShapeDtypeStruct((d, t), BF16),
            jax.ShapeDtypeStruct((nsc, PEER_NKEYS, t), F32),
        ],
        compiler_params=_cparams("parallel"),
        name="merge",
    )(oa, ob, gate, x2, wa, wb, wo, gffn, wq, sk3)


def _pop_max(x, order, exact):
    m = jnp.max(x, axis=0, keepdims=True)
    if exact:
        first = jnp.min(jnp.where(x == m, order, jnp.inf), axis=0, keepdims=True)
        return m, order == first
    return m, x == m


def _topk_ranks(s, exact, with_rank):
    rows = lax.broadcasted_iota(jnp.int32, s.shape, 0).astype(F32)
    rank = jnp.full(s.shape, float(PEER_TOPK), F32) if with_rank else None
    x = s
    vals = []
    for a in range(PEER_TOPK):
        m, hit = _pop_max(x, rows, exact)
        if with_rank:
            rank = jnp.where(hit, float(a), rank)
        x = jnp.where(hit, -jnp.inf, x)
        vals.append(m)
    count = jnp.sum(jnp.where(x == -jnp.inf, 1.0, 0.0), axis=0, keepdims=True)
    return vals, rank, count


def _select_kernel(sc_ref, flat_ref, arow_ref, c1_ref, lim_ref, p2_ref, r2_ref, cand_ref):
    tb = sc_ref.shape[2]
    flat = flat_ref[...]
    arow = arow_ref[...]
    cand_ref[...] = jnp.full(cand_ref.shape, -jnp.inf, F32)

    def head(h, cand, exact):
        s1 = sc_ref[2 * h]
        s2 = sc_ref[2 * h + 1]
        v1, rank1, n1 = _topk_ranks(s1, exact, with_rank=exact)
        v2, rank2, n2 = _topk_ranks(s2, exact, with_rank=True)
        for i, (a, b) in enumerate(_CAND):
            cand[i:i + 1, :] = v1[a] + v2[b]
        top = v1[0] + v2[0]
        sel = jnp.zeros(cand.shape, F32)
        z = jnp.zeros((1, tb), F32)
        x = cand[...]
        for _ in range(PEER_TOPK):
            m, hit = _pop_max(x, flat, exact)
            sel = jnp.where(hit, 1.0, sel)
            x = jnp.where(hit, -jnp.inf, x)
            z = z + jnp.exp(m - top)
        lim = jnp.zeros(s1.shape, F32)
        for a in range(PEER_TOPK):
            bmax = jnp.sum(jnp.where(arow == float(a), sel, 0.0), axis=0, keepdims=True)
            lim = jnp.where((rank1 == float(a)) if exact else (s1 == v1[a]), bmax, lim)
        c1_ref[h] = 0.5 * jnp.exp(s1 - v1[0]) / z
        lim_ref[h] = lim
        p2_ref[h] = jnp.exp(s2 - v2[0]).astype(BF16)
        r2_ref[h] = rank2.astype(BF16)
        nsel = jnp.sum(sel, axis=0, keepdims=True)
        k = float(PEER_TOPK)
        return (n1 != k) | (n2 != k) | (nsel != k)

    def head_pair(i, carry):
        tied = [head(2 * i + j, cand_ref.at[j], exact=False) for j in range(2)]
        for j in range(2):
            @pl.when(jnp.max(jnp.where(tied[j], 1.0, 0.0)) > 0.0)
            def _():
                head(2 * i + j, cand_ref.at[j], exact=True)

        return carry

    lax.fori_loop(0, PEER_HEADS // 2, head_pair, 0)


def _select(sc, tb):
    nsc, nk, t = sc.shape
    flat = np.full((_NCAND_PAD, 1), 1023.0, np.float32)
    arow = np.full((_NCAND_PAD, 1), -1.0, np.float32)
    for i, (a, b) in enumerate(_CAND):
        flat[i, 0] = a * PEER_TOPK + b
        arow[i, 0] = a
    flat = jnp.asarray(np.broadcast_to(flat, (_NCAND_PAD, tb)))
    arow = jnp.asarray(np.broadcast_to(arow, (_NCAND_PAD, tb)))
    out = lambda dt: jax.ShapeDtypeStruct((PEER_HEADS, nk, t), dt)
    ospec = pl.BlockSpec((PEER_HEADS, nk, tb), lambda i: (0, 0, i))
    cspec = pl.BlockSpec((_NCAND_PAD, tb), lambda i: (0, 0))
    return pl.pallas_call(
        _select_kernel,
        grid=(t // tb,),
        in_specs=[pl.BlockSpec((nsc, nk, tb), lambda i: (0, 0, i)), cspec, cspec],
        out_specs=[ospec, ospec, ospec, ospec],
        out_shape=[out(F32), out(F32), out(BF16), out(BF16)],
        scratch_shapes=[pltpu.VMEM((2, _NCAND_PAD, tb), F32)],
        compiler_params=_cparams("parallel"),
        name="select",
    )(sc, flat, arow)


def _experts_kernel(xnt_ref, u_ref, vt_ref, c1_ref, lim_ref, p2_ref, r2_ref, h_ref, gfin_ref, o_ref,
                    acc_ref, g_ref, w_ref):
    e = pl.program_id(1)
    nblk = pl.num_programs(1) - 1
    rows = c1_ref.shape[1]
    nk = PEER_NKEYS

    slot = lax.rem(e, 2)
    g_new = g_ref.at[slot]
    g_old = g_ref.at[1 - slot]

    @pl.when(e == 0)
    def _():
        acc_ref[...] = jnp.zeros_like(acc_ref)
        g_old[...] = jnp.zeros_like(g_old)

    def row_bf16(ref, h, r):
        x = jnp.broadcast_to(ref[h, r:r + 1, :], (16, ref.shape[2])).astype(BF16)
        return jnp.concatenate([x] * (nk // 16), axis=0)

    for r in range(rows):
        w = None
        for h in range(PEER_HEADS):
            term = jnp.where(r2_ref[h] < row_bf16(lim_ref, h, r), p2_ref[h], jnp.zeros((), BF16)) \
                * row_bf16(c1_ref, h, r)
            w = term if w is None else w + term
        w_ref[r * nk:(r + 1) * nk, :] = w

    mc = u_ref.shape[0] // 2
    a = jnp.concatenate([_dot(u_ref[i:i + mc, :], xnt_ref[...]) for i in range(0, u_ref.shape[0], mc)], axis=0)
    acc_ref[...] += _dot(vt_ref[...], g_old[...])
    g_new[...] = (a * (1.0 + lax.erf(a * INV_SQRT2))).astype(BF16) * w_ref[...]

    @pl.when(e == nblk)
    def _():
        hh = h_ref[...] + acc_ref[...].T
        o_ref[...] = hh * lax.rsqrt(jnp.mean(hh * hh, axis=-1, keepdims=True) + EPS) * gfin_ref[...]


def _experts(xnt, u_bf, vt_bf, c1, lim, p2, r2, h, gfin, tt, rows):
    t, d = h.shape
    et = rows * PEER_NKEYS
    nblk = u_bf.shape[0] // et
    cur = lambda e: jnp.minimum(e, nblk - 1)
    prev = lambda e: jnp.maximum(e - 1, 0)
    tok3 = pl.BlockSpec((PEER_HEADS, PEER_NKEYS, tt), lambda i, e: (0, 0, i))
    row3 = pl.BlockSpec((PEER_HEADS, rows, tt), lambda i, e: (0, cur(e), i))
    return pl.pallas_call(
        _experts_kernel,
        grid=(t // tt, nblk + 1),
        in_specs=[
            pl.BlockSpec((d, tt), lambda i, e: (0, i)),
            pl.BlockSpec((et, d), lambda i, e: (cur(e), 0)),
            pl.BlockSpec((d, et), lambda i, e: (0, prev(e))),
            row3, row3, tok3, tok3,
            pl.BlockSpec((tt, d), lambda i, e: (i, 0)),
            pl.BlockSpec((1, d), lambda i, e: (0, 0)),
        ],
        out_specs=pl.BlockSpec((tt, d), lambda i, e: (i, 0)),
        out_shape=jax.ShapeDtypeStruct((t, d), F32),
        scratch_shapes=[pltpu.VMEM((d, tt), F32), pltpu.VMEM((2, et, tt), BF16), pltpu.VMEM((et, tt), BF16)],
        compiler_params=_cparams("parallel", "arbitrary"),
        name="experts",
    )(xnt, u_bf, vt_bf, c1, lim, p2, r2, h, gfin)


def _rope_tables(seq):
    rows = seq // GRID_W
    row = jnp.broadcast_to(jnp.arange(rows)[:, None], (rows, GRID_W)).reshape(-1)
    col = jnp.broadcast_to(jnp.arange(GRID_W)[None, :], (rows, GRID_W)).reshape(-1)
    half = HEAD_DIM // 4
    freqs = ROPE_THETA ** (-jnp.arange(half, dtype=F32) / half)

    def cs(pos):
        ang = pos.astype(F32)[:, None] * freqs[None, :]
        return jnp.cos(ang), jnp.sin(ang)

    cr, sr = cs(row)
    cc, sc = cs(col)
    zero = jnp.zeros_like(sr)
    cos = jnp.concatenate([cr, cr, cc, cc], axis=1)
    sa = jnp.concatenate([-sr, zero, -sc, zero], axis=1)
    sb = jnp.concatenate([zero, sr, zero, sc], axis=1)
    tile = lambda a: jnp.concatenate([a, a], axis=1)
    return tile(cos), tile(sa), tile(sb)


def _bucket_table():
    qi = np.arange(Q_BLOCK)
    kj = np.arange(3 * Q_BLOCK)
    rel = kj[None, :] - Q_BLOCK - qi[:, None]
    nbk = N_BUCKETS // 2
    max_exact = nbk // 2
    assert (max_exact, MAX_DISTANCE, nbk - max_exact) == (8, 128, 8), "the integer form below assumes these constants"
    n = np.abs(rel)
    log_part = np.floor(np.log2(np.maximum(n * n // 64, 1))).astype(np.int64)
    log_part = np.where(2 ** (log_part + 1) * 64 <= n * n, log_part + 1, log_part)
    log_part = np.where(2 ** log_part * 64 > n * n, log_part - 1, log_part)
    large = np.minimum(max_exact + log_part, nbk - 1)
    bucket = np.where(rel > 0, nbk, 0) + np.where(n < max_exact, n, large)
    return jnp.asarray(np.where(n <= WINDOW, bucket, N_BUCKETS).astype(np.int32))


def _tile_sizes(seq, tokens):
    tm = min(512, seq)
    tq = min(128, seq)
    tb = min(128, tokens)
    tt = min(512, tokens)
    return tm, tq, tb, tt


def kernel(x, g_mix, w_in, b_gate, q_norm_g, k_norm_g, rel_bias, sink, w_branch_a, w_branch_b, w_out, g_ffn,
           peer_wq, peer_subkeys, peer_u, peer_v, g_final):
    bsz, seq, d = x.shape
    t = bsz * seq
    depth = w_in.shape[0]
    tm, tq, tb, tt = _tile_sizes(seq, t)
    assert depth == 1, "the experts stage fuses the final norm, so only a single layer is supported"
    assert seq % tm == 0 and seq % Q_BLOCK == 0 and t % tb == 0 and t % tt == 0

    cos, sa, sb = _rope_tables(seq)
    bucket = _bucket_table()
    bd = jnp.asarray(np.kron(np.eye(2, dtype=np.float32),
                             np.full((HEAD_DIM, HEAD_DIM), 1.0 / HEAD_DIM, np.float32))).astype(BF16)
    expert_rows = 8

    h = x.reshape(t, d)
    for l in range(depth):
        qg = jnp.tile(q_norm_g[l], 2)[None, :]
        kg = jnp.tile(k_norm_g[l], 2)[None, :]
        qa, ka, va, qb, kb, vb, gate = _in_proj(
            h, g_mix[l][None, :], w_in[l].astype(BF16), b_gate[l][None, :], qg, kg, cos, sa, sb, bd, seq, tm)
        oa = _attn_a(qa, ka, va, bsz, seq, tq)
        pad = lambda a: jnp.pad(a.reshape(B_KV, bsz, seq, a.shape[-1]), ((0, 0), (0, 0), (Q_BLOCK, Q_BLOCK), (0, 0)))
        ob = _attn_b(qb, pad(kb), pad(vb), bucket, rel_bias, sink[l][None, :], bsz, seq)
        skh, skl = _split_bf16(peer_subkeys[l].reshape(2 * PEER_HEADS, PEER_NKEYS, PEER_DQ // 2))
        sk3 = jnp.concatenate([skh, skh, skl], axis=-1)
        h1, xn, sc = _merge(oa, ob, gate, h, w_branch_a[l].astype(BF16), w_branch_b[l].astype(BF16),
                            w_out[l].astype(BF16), g_ffn[l][None, :], peer_wq[l].astype(BF16), sk3, tm)
        c1, lim, p2, r2 = _select(sc, tb)
        out = _experts(xn, peer_u[l].astype(BF16), peer_v[l].astype(BF16).T, c1, lim, p2, r2, h1,
                       g_final[None, :], tt, expert_rows)
    return out.reshape(bsz, seq, d)
```

```python
import functools
import math

import numpy as np
import jax
import jax.numpy as jnp
from jax import lax
from jax.experimental import pallas as pl
from jax.experimental.pallas import tpu as pltpu

F32 = jnp.float32
BF16 = jnp.bfloat16

HEAD_DIM = 64
A_HEADS, A_KV = 8, 2
B_HEADS, B_KV = 8, 2
Q_BLOCK = 128
WINDOW = 128
GRID_W = 64
ROPE_THETA = 10000.0
N_BUCKETS = 32
MAX_DISTANCE = 128
PEER_HEADS = 8
PEER_NKEYS = 128
PEER_TOPK = 16
PEER_DQ = 256
EPS = 1e-6
NEG = -1e30
INV_SQRT2 = 0.7071067811865476
LOG2E = 1.4426950408889634

V7X_LANES = 128
V7X_VMEM_LIMIT_BYTES = 56 * 1024 * 1024

_CAND = [(a, b) for a in range(PEER_TOPK) for b in range(PEER_TOPK) if (a + 1) * (b + 1) <= PEER_TOPK]
_NCAND = len(_CAND)
_NCAND_PAD = -(-_NCAND // 8) * 8


def _cparams(*sem):
    return pltpu.CompilerParams(dimension_semantics=sem, vmem_limit_bytes=V7X_VMEM_LIMIT_BYTES)


def _split_bf16(x):
    hi = x.astype(BF16)
    lo = (x - hi.astype(F32)).astype(BF16)
    return hi, lo


def _dot(a, b):
    return jnp.dot(a, b, preferred_element_type=F32)


def _dot_nt(a, b):
    return lax.dot_general(a, b, (((1,), (1,)), ((), ())), preferred_element_type=F32)


def _in_proj_kernel(x_ref, gmix_ref, w_ref, bg_ref, qg_ref, kg_ref, cos_ref, sa_ref, sb_ref, bd_ref,
                    qa_ref, ka_ref, va_ref, qb_ref, kb_ref, vb_ref, gate_ref):
    d = x_ref.shape[1]
    x = x_ref[...]
    n = x * lax.rsqrt(jnp.mean(x * x, axis=-1, keepdims=True) + EPS) * gmix_ref[...]
    nb = n.astype(BF16)
    cos, sa, sb, bd = cos_ref[...], sa_ref[...], sb_ref[...], bd_ref[...]

    def norm_rope(q, gain):
        hi, lo = _split_bf16(q * q)
        ms = _dot(hi, bd) + _dot(lo, bd)
        y = q * lax.rsqrt(ms + EPS) * gain
        return y * cos + pltpu.roll(y, V7X_LANES - 16, 1) * sa + pltpu.roll(y, 16, 1) * sb

    def put_heads(ref, first, y):
        ref[first] = y[:, :HEAD_DIM].astype(BF16)
        ref[first + 1] = y[:, HEAD_DIM:].astype(BF16)

    def put_values(ref, y):
        lane = lax.broadcasted_iota(jnp.int32, y.shape, 1)
        tail = jnp.where(lane == HEAD_DIM, 1.0, 0.0)
        ref[0] = jnp.where(lane < HEAD_DIM, y, tail).astype(BF16)
        ref[1] = jnp.where(lane < HEAD_DIM, pltpu.roll(y, HEAD_DIM, 1), tail).astype(BF16)

    qscale = HEAD_DIM ** -0.5 * LOG2E
    n_attn = (A_HEADS + 2 * A_KV + B_HEADS + 2 * B_KV) * HEAD_DIM
    z = _dot(nb, w_ref[:, :n_attn])
    col = lambda c: z[:, c * 128:(c + 1) * 128]
    c = 0
    for j in range(A_HEADS // 2):
        put_heads(qa_ref, 2 * j, norm_rope(col(c), qg_ref[...]) * qscale)
        c += 1
    put_heads(ka_ref, 0, norm_rope(col(c), kg_ref[...]))
    put_values(va_ref, col(c + 1))
    c += 2
    for j in range(B_HEADS // 2):
        put_heads(qb_ref, 2 * j, col(c) * qscale)
        c += 1
    put_heads(kb_ref, 0, col(c))
    put_values(vb_ref, col(c + 1))
    for j in range(2 * d // 512):
        lo, hi = n_attn + j * 512, n_attn + (j + 1) * 512
        zg = _dot(nb, w_ref[:, lo:hi]) + bg_ref[:, j * 512:(j + 1) * 512]
        gate_ref[:, j * 512:(j + 1) * 512] = (1.0 / (1.0 + jnp.exp(-zg))).astype(BF16)


def _in_proj(x2, gmix, w_bf, bg, qg, kg, cos, sa, sb, bd, seq, tm):
    t, d = x2.shape
    nt = t // tm
    ns = seq // tm
    const = lambda i: (0, 0)
    tab = pl.BlockSpec((tm, 128), lambda i: (i % ns, 0))
    head = lambda nh: pl.BlockSpec((nh, tm, HEAD_DIM), lambda i: (0, i, 0))
    wide = lambda nh: pl.BlockSpec((nh, tm, 2 * HEAD_DIM), lambda i: (0, i, 0))
    return pl.pallas_call(
        _in_proj_kernel,
        grid=(nt,),
        in_specs=[
            pl.BlockSpec((tm, d), lambda i: (i, 0)),
            pl.BlockSpec((1, d), const),
            pl.BlockSpec(w_bf.shape, const),
            pl.BlockSpec((1, 2 * d), const),
            pl.BlockSpec((1, 128), const),
            pl.BlockSpec((1, 128), const),
            tab, tab, tab,
            pl.BlockSpec((128, 128), const),
        ],
        out_specs=[head(A_HEADS), head(A_KV), wide(A_KV), head(B_HEADS), head(B_KV), wide(B_KV),
                   pl.BlockSpec((tm, 2 * d), lambda i: (i, 0))],
        out_shape=[
            jax.ShapeDtypeStruct((A_HEADS, t, HEAD_DIM), BF16),
            jax.ShapeDtypeStruct((A_KV, t, HEAD_DIM), BF16),
            jax.ShapeDtypeStruct((A_KV, t, 2 * HEAD_DIM), BF16),
            jax.ShapeDtypeStruct((B_HEADS, t, HEAD_DIM), BF16),
            jax.ShapeDtypeStruct((B_KV, t, HEAD_DIM), BF16),
            jax.ShapeDtypeStruct((B_KV, t, 2 * HEAD_DIM), BF16),
            jax.ShapeDtypeStruct((t, 2 * d), BF16),
        ],
        compiler_params=_cparams("parallel"),
        name="in_proj",
    )(x2, gmix, w_bf, bg, qg, kg, cos, sa, sb, bd)


def _attn_a_kernel(q_ref, k_ref, v_ref, o_ref, *, kc):
    r, tq, hd = q_ref.shape
    seq = k_ref.shape[0]
    q = q_ref[...].reshape(r * tq, hd)
    m = jnp.full((r * tq, 1), -jnp.inf, F32)
    acc = jnp.zeros((r * tq, v_ref.shape[1]), F32)
    for c in range(seq // kc):
        s = _dot_nt(q, k_ref[c * kc:(c + 1) * kc, :])
        m_new = jnp.maximum(m, jnp.max(s, axis=-1, keepdims=True))
        p = jnp.exp2(s - m_new)
        acc = jnp.exp2(m - m_new) * acc + _dot(p.astype(BF16), v_ref[c * kc:(c + 1) * kc, :])
        m = m_new
    o = acc[:, :hd] / acc[:, hd:hd + 1]
    for j in range(r):
        o_ref[:, j * hd:(j + 1) * hd] = o[j * tq:(j + 1) * tq].astype(BF16)


def _attn_a(qa, ka, va, bsz, seq, tq):
    t = qa.shape[1]
    r = A_HEADS // A_KV
    nq = seq // tq
    return pl.pallas_call(
        functools.partial(_attn_a_kernel, kc=min(512, seq)),
        grid=(bsz, A_KV, nq),
        in_specs=[
            pl.BlockSpec((r, tq, HEAD_DIM), lambda b, g, i: (g, b * nq + i, 0)),
            pl.BlockSpec((None, seq, HEAD_DIM), lambda b, g, i: (g, b, 0)),
            pl.BlockSpec((None, seq, 2 * HEAD_DIM), lambda b, g, i: (g, b, 0)),
        ],
        out_specs=pl.BlockSpec((tq, r * HEAD_DIM), lambda b, g, i: (b * nq + i, g)),
        out_shape=jax.ShapeDtypeStruct((t, A_HEADS * HEAD_DIM), BF16),
        compiler_params=_cparams("parallel", "parallel", "parallel"),
        name="attn_a",
    )(qa, ka, va)


def _attn_b_kernel(relb_ref, sink_ref, q_ref, k_ref, v_ref, bucket_ref, o_ref, bias_ref):
    b = pl.program_id(0)
    n = pl.program_id(1)
    nblk = pl.num_programs(1)
    r = B_HEADS // B_KV
    qb = Q_BLOCK
    kw = 3 * Q_BLOCK

    @pl.when(jnp.logical_and(b == 0, n == 0))
    def _():
        bucket = bucket_ref[...]
        for h in range(B_HEADS):
            acc = jnp.full((qb, kw), NEG, F32)
            for k in range(N_BUCKETS):
                acc = jnp.where(bucket == k, relb_ref[k, h] * LOG2E, acc)
            bias_ref[h * qb:(h + 1) * qb, :] = acc

    nq = q_ref.shape[1] // qb
    kj = lax.broadcasted_iota(jnp.int32, (1, kw), 1)
    for i in range(nq):
        blk = n * nq + i
        kpos = blk * qb - qb + kj
        inside = (kpos >= 0) & (kpos < nblk * nq * qb)
        start = pl.multiple_of(blk * qb, qb)
        for g in range(B_KV):
            kwin = k_ref[g, pl.ds(start, kw), :]
            vwin = v_ref[g, pl.ds(start, kw), :]
            q = q_ref[g * r:(g + 1) * r, i * qb:(i + 1) * qb, :].reshape(r * qb, HEAD_DIM)
            sc = _dot_nt(q, kwin) + bias_ref[g * r * qb:(g + 1) * r * qb, :]
            sc = jnp.where(inside, sc, NEG)
            sink = jnp.concatenate(
                [jnp.full((qb, 1), sink_ref[0, g * r + j] * LOG2E, F32) for j in range(r)], axis=0)
            m = jnp.maximum(jnp.max(sc, axis=-1, keepdims=True), sink)
            pv = _dot(jnp.exp2(sc - m).astype(BF16), vwin)
            o = pv[:, :HEAD_DIM] / (pv[:, HEAD_DIM:HEAD_DIM + 1] + jnp.exp2(sink - m))
            for j in range(r):
                h = g * r + j
                o_ref[i * qb:(i + 1) * qb, h * HEAD_DIM:(h + 1) * HEAD_DIM] = o[j * qb:(j + 1) * qb].astype(BF16)


def _attn_b(qb, kb_pad, vb_pad, bucket, rel_bias, sink, bsz, seq):
    t = qb.shape[1]
    nb = seq // Q_BLOCK
    sp = seq + 2 * Q_BLOCK
    nq = 2 if nb % 2 == 0 else 1
    ns = nb // nq
    return pl.pallas_call(
        _attn_b_kernel,
        grid=(bsz, ns),
        in_specs=[
            pl.BlockSpec(memory_space=pltpu.SMEM),
            pl.BlockSpec(memory_space=pltpu.SMEM),
            pl.BlockSpec((B_HEADS, nq * Q_BLOCK, HEAD_DIM), lambda b, n: (0, b * ns + n, 0)),
            pl.BlockSpec((B_KV, None, sp, HEAD_DIM), lambda b, n: (0, b, 0, 0)),
            pl.BlockSpec((B_KV, None, sp, 2 * HEAD_DIM), lambda b, n: (0, b, 0, 0)),
            pl.BlockSpec((Q_BLOCK, 3 * Q_BLOCK), lambda b, n: (0, 0)),
        ],
        out_specs=pl.BlockSpec((nq * Q_BLOCK, B_HEADS * HEAD_DIM), lambda b, n: (b * ns + n, 0)),
        out_shape=jax.ShapeDtypeStruct((t, B_HEADS * HEAD_DIM), BF16),
        scratch_shapes=[pltpu.VMEM((B_HEADS * Q_BLOCK, 3 * Q_BLOCK), F32)],
        compiler_params=_cparams("arbitrary", "arbitrary"),
        name="attn_b",
    )(rel_bias, sink, qb, kb_pad, vb_pad, bucket)


def _merge_kernel(oa_ref, ob_ref, gate_ref, x_ref, wa_ref, wb_ref, wo_ref, gffn_ref, wq_ref, sk3_ref,
                  h_ref, xnt_ref, sc_ref):
    d = x_ref.shape[1]
    pa = _dot(oa_ref[...], wa_ref[...])
    pb = _dot(ob_ref[...], wb_ref[...])
    mix = gate_ref[:, :d].astype(F32) * pa + gate_ref[:, d:].astype(F32) * pb
    h = x_ref[...] + _dot(mix.astype(BF16), wo_ref[...])
    h_ref[...] = h
    xn = h * lax.rsqrt(jnp.mean(h * h, axis=-1, keepdims=True) + EPS) * gffn_ref[...]
    xnb = xn.astype(BF16)
    xnt_ref[...] = xn.T.astype(BF16)
    half = PEER_DQ // 2
    qh, ql = _split_bf16(_dot(xnb, wq_ref[...]))
    for hc in range(2 * PEER_HEADS):
        cols = slice(hc * half, (hc + 1) * half)
        q3 = jnp.concatenate([qh[:, cols], ql[:, cols], qh[:, cols]], axis=1)
        sc_ref[hc] = _dot_nt(sk3_ref[hc], q3)


def _merge(oa, ob, gate, x2, wa, wb, wo, gffn, wq, sk3, tm):
    t, d = x2.shape
    const2 = lambda i: (0, 0)
    const3 = lambda i: (0, 0, 0)
    row = lambda w: pl.BlockSpec((tm, w), lambda i: (i, 0))
    nsc = 2 * PEER_HEADS
    return pl.pallas_call(
        _merge_kernel,
        grid=(t // tm,),
        in_specs=[
            row(oa.shape[1]), row(ob.shape[1]), row(2 * d), row(d),
            pl.BlockSpec(wa.shape, const2), pl.BlockSpec(wb.shape, const2), pl.BlockSpec(wo.shape, const2),
            pl.BlockSpec((1, d), const2), pl.BlockSpec(wq.shape, const2),
            pl.BlockSpec(sk3.shape, const3),
        ],
        out_specs=[row(d), pl.BlockSpec((d, tm), lambda i: (0, i)),
                   pl.BlockSpec((nsc, PEER_NKEYS, tm), lambda i: (0, 0, i))],
        out_shape=[
            jax.ShapeDtypeStruct((t, d), F32),
            jax.ShapeDtypeStruct((d, t), BF16),
            jax.ShapeDtypeStruct((nsc, PEER_NKEYS, t), F32),
        ],
        compiler_params=_cparams("parallel"),
        name="merge",
    )(oa, ob, gate, x2, wa, wb, wo, gffn, wq, sk3)


def _pop_max(x, order, exact):
    m = jnp.max(x, axis=0, keepdims=True)
    if exact:
        first = jnp.min(jnp.where(x == m, order, jnp.inf), axis=0, keepdims=True)
        return m, order == first
    return m, x == m


def _topk_ranks(s, exact, with_rank):
    rows = lax.broadcasted_iota(jnp.int32, s.shape, 0).astype(F32)
    rank = jnp.full(s.shape, float(PEER_TOPK), F32) if with_rank else None
    x = s
    vals = []
    for a in range(PEER_TOPK):
        m, hit = _pop_max(x, rows, exact)
        if with_rank:
            rank = jnp.where(hit, float(a), rank)
        x = jnp.where(hit, -jnp.inf, x)
        vals.append(m)
    count = jnp.sum(jnp.where(x == -jnp.inf, 1.0, 0.0), axis=0, keepdims=True)
    return vals, rank, count


def _select_kernel(sc_ref, flat_ref, arow_ref, c1_ref, lim_ref, p2_ref, r2_ref, cand_ref):
    tb = sc_ref.shape[2]
    flat = flat_ref[...]
    arow = arow_ref[...]
    cand_ref[...] = jnp.full(cand_ref.shape, -jnp.inf, F32)

    def head(h, cand, exact):
        s1 = sc_ref[2 * h]
        s2 = sc_ref[2 * h + 1]
        v1, rank1, n1 = _topk_ranks(s1, exact, with_rank=exact)
        v2, rank2, n2 = _topk_ranks(s2, exact, with_rank=True)
        for i, (a, b) in enumerate(_CAND):
            cand[i:i + 1, :] = v1[a] + v2[b]
        top = v1[0] + v2[0]
        sel = jnp.zeros(cand.shape, F32)
        z = jnp.zeros((1, tb), F32)
        x = cand[...]
        for _ in range(PEER_TOPK):
            m, hit = _pop_max(x, flat, exact)
            sel = jnp.where(hit, 1.0, sel)
            x = jnp.where(hit, -jnp.inf, x)
            z = z + jnp.exp(m - top)
        lim = jnp.zeros(s1.shape, F32)
        for a in range(PEER_TOPK):
            bmax = jnp.sum(jnp.where(arow == float(a), sel, 0.0), axis=0, keepdims=True)
            lim = jnp.where((rank1 == float(a)) if exact else (s1 == v1[a]), bmax, lim)
        c1_ref[h] = 0.5 * jnp.exp(s1 - v1[0]) / z
        lim_ref[h] = lim
        p2_ref[h] = jnp.exp(s2 - v2[0]).astype(BF16)
        r2_ref[h] = rank2.astype(BF16)
        nsel = jnp.sum(sel, axis=0, keepdims=True)
        k = float(PEER_TOPK)
        return (n1 != k) | (n2 != k) | (nsel != k)

    def head_pair(i, carry):
        tied = [head(2 * i + j, cand_ref.at[j], exact=False) for j in range(2)]
        for j in range(2):
            @pl.when(jnp.max(jnp.where(tied[j], 1.0, 0.0)) > 0.0)
            def _():
                head(2 * i + j, cand_ref.at[j], exact=True)

        return carry

    lax.fori_loop(0, PEER_HEADS // 2, head_pair, 0)


def _select(sc, tb):
    nsc, nk, t = sc.shape
    flat = np.full((_NCAND_PAD, 1), 1023.0, np.float32)
    arow = np.full((_NCAND_PAD, 1), -1.0, np.float32)
    for i, (a, b) in enumerate(_CAND):
        flat[i, 0] = a * PEER_TOPK + b
        arow[i, 0] = a
    flat = jnp.asarray(np.broadcast_to(flat, (_NCAND_PAD, tb)))
    arow = jnp.asarray(np.broadcast_to(arow, (_NCAND_PAD, tb)))
    out = lambda dt: jax.ShapeDtypeStruct((PEER_HEADS, nk, t), dt)
    ospec = pl.BlockSpec((PEER_HEADS, nk, tb), lambda i: (0, 0, i))
    cspec = pl.BlockSpec((_NCAND_PAD, tb), lambda i: (0, 0))
    return pl.pallas_call(
        _select_kernel,
        grid=(t // tb,),
        in_specs=[pl.BlockSpec((nsc, nk, tb), lambda i: (0, 0, i)), cspec, cspec],
        out_specs=[ospec, ospec, ospec, ospec],
        out_shape=[out(F32), out(F32), out(BF16), out(BF16)],
        scratch_shapes=[pltpu.VMEM((2, _NCAND_PAD, tb), F32)],
        compiler_params=_cparams("parallel"),
        name="select",
    )(sc, flat, arow)


def _experts_kernel(xnt_ref, u_ref, vt_ref, c1_ref, lim_ref, p2_ref, r2_ref, h_ref, gfin_ref, o_ref,
                    acc_ref, g_ref, w_ref, *, nblk):
    s = pl.program_id(0)
    e = lax.rem(s, nblk)
    rows = c1_ref.shape[1]
    nk = PEER_NKEYS

    slot = lax.rem(s, 2)
    g_new = g_ref.at[slot]
    g_old = g_ref.at[1 - slot]

    @pl.when(s == 0)
    def _():
        acc_ref[...] = jnp.zeros_like(acc_ref)
        g_old[...] = jnp.zeros_like(g_old)

    def row_bf16(ref, h, r):
        x = jnp.broadcast_to(ref[h, r:r + 1, :], (16, ref.shape[2])).astype(BF16)
        return jnp.concatenate([x] * (nk // 16), axis=0)

    for r in range(rows):
        w = None
        for h in range(PEER_HEADS):
            term = jnp.where(r2_ref[h] < row_bf16(lim_ref, h, r), p2_ref[h], jnp.zeros((), BF16)) \
                * row_bf16(c1_ref, h, r)
            w = term if w is None else w + term
        w_ref[r * nk:(r + 1) * nk, :] = w

    mc = min(512, u_ref.shape[0])
    a = jnp.concatenate([_dot(u_ref[i:i + mc, :], xnt_ref[...]) for i in range(0, u_ref.shape[0], mc)], axis=0)
    contrib = _dot(vt_ref[...], g_old[...])
    acc_ref[...] = jnp.where(lax.rem(s + nblk - 1, nblk) == 0, contrib, acc_ref[...] + contrib)
    g_new[...] = (a * (1.0 + lax.erf(a * INV_SQRT2))).astype(BF16) * w_ref[...]

    @pl.when(jnp.logical_and(e == 0, s > 0))
    def _():
        hh = h_ref[...] + acc_ref[...].T
        o_ref[...] = hh * lax.rsqrt(jnp.mean(hh * hh, axis=-1, keepdims=True) + EPS) * gfin_ref[...]


def _experts(xnt, u_bf, vt_blk, c1, lim, p2, r2, h, gfin, tt):
    t, d = h.shape
    nblk, _, et = vt_blk.shape
    rows = et // PEER_NKEYS
    nt = t // tt
    tile = lambda s: jnp.minimum(s // nblk, nt - 1)
    blk = lambda s: lax.rem(s, nblk)
    prev_tile = lambda s: jnp.maximum(s - 1, 0) // nblk
    prev_blk = lambda s: lax.rem(s + nblk - 1, nblk)
    tok3 = pl.BlockSpec((PEER_HEADS, PEER_NKEYS, tt), lambda s: (0, 0, tile(s)))
    row3 = pl.BlockSpec((PEER_HEADS, rows, tt), lambda s: (0, blk(s), tile(s)))
    return pl.pallas_call(
        functools.partial(_experts_kernel, nblk=nblk),
        grid=(nt * nblk + 1,),
        in_specs=[
            pl.BlockSpec((d, tt), lambda s: (0, tile(s))),
            pl.BlockSpec((et, d), lambda s: (blk(s), 0)),
            pl.BlockSpec((None, d, et), lambda s: (prev_blk(s), 0, 0)),
            row3, row3, tok3, tok3,
            pl.BlockSpec((tt, d), lambda s: (prev_tile(s), 0)),
            pl.BlockSpec((1, d), lambda s: (0, 0)),
        ],
        out_specs=pl.BlockSpec((tt, d), lambda s: (prev_tile(s), 0)),
        out_shape=jax.ShapeDtypeStruct((t, d), F32),
        scratch_shapes=[pltpu.VMEM((d, tt), F32), pltpu.VMEM((2, et, tt), BF16), pltpu.VMEM((et, tt), BF16)],
        compiler_params=_cparams("arbitrary"),
        name="experts",
    )(xnt, u_bf, vt_blk, c1, lim, p2, r2, h, gfin)


def _rope_tables(seq):
    rows = seq // GRID_W
    row = np.broadcast_to(np.arange(rows)[:, None], (rows, GRID_W)).reshape(-1)
    col = np.broadcast_to(np.arange(GRID_W)[None, :], (rows, GRID_W)).reshape(-1)
    half = HEAD_DIM // 4
    freqs = ROPE_THETA ** (-np.arange(half, dtype=np.float64) / half)

    def cs(pos):
        ang = pos.astype(np.float64)[:, None] * freqs[None, :]
        return np.cos(ang), np.sin(ang)

    cr, sr = cs(row)
    cc, sc = cs(col)
    zero = np.zeros_like(sr)
    cos = np.concatenate([cr, cr, cc, cc], axis=1)
    sa = np.concatenate([-sr, zero, -sc, zero], axis=1)
    sb = np.concatenate([zero, sr, zero, sc], axis=1)
    tile = lambda a: jnp.asarray(np.concatenate([a, a], axis=1).astype(np.float32))
    return tile(cos), tile(sa), tile(sb)


def _bucket_table():
    qi = np.arange(Q_BLOCK)
    kj = np.arange(3 * Q_BLOCK)
    rel = kj[None, :] - Q_BLOCK - qi[:, None]
    nbk = N_BUCKETS // 2
    max_exact = nbk // 2
    assert (max_exact, MAX_DISTANCE, nbk - max_exact) == (8, 128, 8), "the integer form below assumes these constants"
    n = np.abs(rel)
    log_part = np.floor(np.log2(np.maximum(n * n // 64, 1))).astype(np.int64)
    log_part = np.where(2 ** (log_part + 1) * 64 <= n * n, log_part + 1, log_part)
    log_part = np.where(2 ** log_part * 64 > n * n, log_part - 1, log_part)
    large = np.minimum(max_exact + log_part, nbk - 1)
    bucket = np.where(rel > 0, nbk, 0) + np.where(n < max_exact, n, large)
    return jnp.asarray(np.where(n <= WINDOW, bucket, N_BUCKETS).astype(np.int32))


def _tile_sizes(seq, tokens):
    tm = min(512, seq)
    tq = min(512, seq)
    tb = min(128, tokens)
    tt = min(512, tokens)
    return tm, tq, tb, tt


def kernel(x, g_mix, w_in, b_gate, q_norm_g, k_norm_g, rel_bias, sink, w_branch_a, w_branch_b, w_out, g_ffn,
           peer_wq, peer_subkeys, peer_u, peer_v, g_final):
    bsz, seq, d = x.shape
    t = bsz * seq
    depth = w_in.shape[0]
    tm, tq, tb, tt = _tile_sizes(seq, t)
    assert depth == 1, "the experts stage fuses the final norm, so only a single layer is supported"
    assert seq % tm == 0 and seq % Q_BLOCK == 0 and t % tb == 0 and t % tt == 0

    cos, sa, sb = _rope_tables(seq)
    bucket = _bucket_table()
    bd = jnp.asarray(np.kron(np.eye(2, dtype=np.float32),
                             np.full((HEAD_DIM, HEAD_DIM), 1.0 / HEAD_DIM, np.float32))).astype(BF16)
    expert_block = 2048

    h = x.reshape(t, d)
    for l in range(depth):
        qg = jnp.tile(q_norm_g[l], 2)[None, :]
        kg = jnp.tile(k_norm_g[l], 2)[None, :]
        qa, ka, va, qb, kb, vb, gate = _in_proj(
            h, g_mix[l][None, :], w_in[l].astype(BF16), b_gate[l][None, :], qg, kg, cos, sa, sb, bd, seq, tm)
        oa = _attn_a(qa, ka, va, bsz, seq, tq)
        pad = lambda a: jnp.pad(a.reshape(B_KV, bsz, seq, a.shape[-1]), ((0, 0), (0, 0), (Q_BLOCK, Q_BLOCK), (0, 0)))
        ob = _attn_b(qb, pad(kb), pad(vb), bucket, rel_bias, sink[l][None, :], bsz, seq)
        skh, skl = _split_bf16(peer_subkeys[l].reshape(2 * PEER_HEADS, PEER_NKEYS, PEER_DQ // 2))
        sk3 = jnp.concatenate([skh, skh, skl], axis=-1)
        h1, xnt, sc = _merge(oa, ob, gate, h, w_branch_a[l].astype(BF16), w_branch_b[l].astype(BF16),
                            w_out[l].astype(BF16), g_ffn[l][None, :], peer_wq[l].astype(BF16), sk3, tm)
        c1, lim, p2, r2 = _select(sc, tb)
        vt_blk = peer_v[l].astype(BF16).reshape(-1, expert_block, d).transpose(0, 2, 1)
        out = _experts(xnt, peer_u[l].astype(BF16), vt_blk, c1, lim, p2, r2, h1, g_final[None, :], tt)
    return out.reshape(bsz, seq, d)
```

```python
import functools
import math

import numpy as np
import jax
import jax.numpy as jnp
from jax import lax
from jax.experimental import pallas as pl
from jax.experimental.pallas import tpu as pltpu

F32 = jnp.float32
BF16 = jnp.bfloat16

HEAD_DIM = 64
A_HEADS, A_KV = 8, 2
B_HEADS, B_KV = 8, 2
Q_BLOCK = 128
WINDOW = 128
GRID_W = 64
ROPE_THETA = 10000.0
N_BUCKETS = 32
MAX_DISTANCE = 128
PEER_HEADS = 8
PEER_NKEYS = 128
PEER_TOPK = 16
PEER_DQ = 256
EPS = 1e-6
NEG = -1e30
INV_SQRT2 = 0.7071067811865476
LOG2E = 1.4426950408889634

V7X_LANES = 128
V7X_VMEM_LIMIT_BYTES = 56 * 1024 * 1024

_CAND = [(a, b) for a in range(PEER_TOPK) for b in range(PEER_TOPK) if (a + 1) * (b + 1) <= PEER_TOPK]
_NCAND = len(_CAND)
_NCAND_PAD = -(-_NCAND // 8) * 8


def _cparams(*sem):
    return pltpu.CompilerParams(dimension_semantics=sem, vmem_limit_bytes=V7X_VMEM_LIMIT_BYTES)


def _split_bf16(x):
    hi = x.astype(BF16)
    lo = (x - hi.astype(F32)).astype(BF16)
    return hi, lo


def _dot(a, b):
    return jnp.dot(a, b, preferred_element_type=F32)


def _dot_nt(a, b):
    return lax.dot_general(a, b, (((1,), (1,)), ((), ())), preferred_element_type=F32)


def _in_proj_kernel(x_ref, gmix_ref, w_ref, bg_ref, qg_ref, kg_ref, cos_ref, sa_ref, sb_ref, bd_ref,
                    qa_ref, ka_ref, va_ref, qb_ref, kb_ref, vb_ref, gate_ref):
    d = x_ref.shape[1]
    x = x_ref[...]
    n = x * lax.rsqrt(jnp.mean(x * x, axis=-1, keepdims=True) + EPS) * gmix_ref[...]
    nb = n.astype(BF16)
    cos, sa, sb, bd = cos_ref[...], sa_ref[...], sb_ref[...], bd_ref[...]

    def norm_rope(q, gain):
        hi, lo = _split_bf16(q * q)
        ms = _dot(hi, bd) + _dot(lo, bd)
        y = q * lax.rsqrt(ms + EPS) * gain
        return y * cos + pltpu.roll(y, V7X_LANES - 16, 1) * sa + pltpu.roll(y, 16, 1) * sb

    def put_heads(ref, first, y):
        ref[first] = y[:, :HEAD_DIM].astype(BF16)
        ref[first + 1] = y[:, HEAD_DIM:].astype(BF16)

    def put_values(ref, y):
        lane = lax.broadcasted_iota(jnp.int32, y.shape, 1)
        tail = jnp.where(lane == HEAD_DIM, 1.0, 0.0)
        ref[0] = jnp.where(lane < HEAD_DIM, y, tail).astype(BF16)
        ref[1] = jnp.where(lane < HEAD_DIM, pltpu.roll(y, HEAD_DIM, 1), tail).astype(BF16)

    qscale = HEAD_DIM ** -0.5 * LOG2E
    n_attn = (A_HEADS + 2 * A_KV + B_HEADS + 2 * B_KV) * HEAD_DIM
    z = _dot(nb, w_ref[:, :n_attn])
    col = lambda c: z[:, c * 128:(c + 1) * 128]
    c = 0
    for j in range(A_HEADS // 2):
        put_heads(qa_ref, 2 * j, norm_rope(col(c), qg_ref[...]) * qscale)
        c += 1
    put_heads(ka_ref, 0, norm_rope(col(c), kg_ref[...]))
    put_values(va_ref, col(c + 1))
    c += 2
    for j in range(B_HEADS // 2):
        put_heads(qb_ref, 2 * j, col(c) * qscale)
        c += 1
    put_heads(kb_ref, 0, col(c))
    put_values(vb_ref, col(c + 1))
    for j in range(2 * d // 512):
        lo, hi = n_attn + j * 512, n_attn + (j + 1) * 512
        zg = _dot(nb, w_ref[:, lo:hi]) + bg_ref[:, j * 512:(j + 1) * 512]
        gate_ref[:, j * 512:(j + 1) * 512] = (1.0 / (1.0 + jnp.exp(-zg))).astype(BF16)


def _in_proj(x2, gmix, w_bf, bg, qg, kg, cos, sa, sb, bd, seq, tm):
    t, d = x2.shape
    nt = t // tm
    ns = seq // tm
    const = lambda i: (0, 0)
    tab = pl.BlockSpec((tm, 128), lambda i: (i % ns, 0))
    head = lambda nh: pl.BlockSpec((nh, tm, HEAD_DIM), lambda i: (0, i, 0))
    wide = lambda nh: pl.BlockSpec((nh, tm, 2 * HEAD_DIM), lambda i: (0, i, 0))
    return pl.pallas_call(
        _in_proj_kernel,
        grid=(nt,),
        in_specs=[
            pl.BlockSpec((tm, d), lambda i: (i, 0)),
            pl.BlockSpec((1, d), const),
            pl.BlockSpec(w_bf.shape, const),
            pl.BlockSpec((1, 2 * d), const),
            pl.BlockSpec((1, 128), const),
            pl.BlockSpec((1, 128), const),
            tab, tab, tab,
            pl.BlockSpec((128, 128), const),
        ],
        out_specs=[head(A_HEADS), head(A_KV), wide(A_KV), head(B_HEADS), head(B_KV), wide(B_KV),
                   pl.BlockSpec((tm, 2 * d), lambda i: (i, 0))],
        out_shape=[
            jax.ShapeDtypeStruct((A_HEADS, t, HEAD_DIM), BF16),
            jax.ShapeDtypeStruct((A_KV, t, HEAD_DIM), BF16),
            jax.ShapeDtypeStruct((A_KV, t, 2 * HEAD_DIM), BF16),
            jax.ShapeDtypeStruct((B_HEADS, t, HEAD_DIM), BF16),
            jax.ShapeDtypeStruct((B_KV, t, HEAD_DIM), BF16),
            jax.ShapeDtypeStruct((B_KV, t, 2 * HEAD_DIM), BF16),
            jax.ShapeDtypeStruct((t, 2 * d), BF16),
        ],
        compiler_params=_cparams("parallel"),
        name="in_proj",
    )(x2, gmix, w_bf, bg, qg, kg, cos, sa, sb, bd)


def _attn_a_kernel(q_ref, k_ref, v_ref, o_ref, *, kc):
    r, tq, hd = q_ref.shape
    seq = k_ref.shape[0]
    q = q_ref[...].reshape(r * tq, hd)
    m = jnp.full((r * tq, 1), -jnp.inf, F32)
    acc = jnp.zeros((r * tq, v_ref.shape[1]), F32)
    for c in range(seq // kc):
        s = _dot_nt(q, k_ref[c * kc:(c + 1) * kc, :])
        m_new = jnp.maximum(m, jnp.max(s, axis=-1, keepdims=True))
        p = jnp.exp2(s - m_new)
        acc = jnp.exp2(m - m_new) * acc + _dot(p.astype(BF16), v_ref[c * kc:(c + 1) * kc, :])
        m = m_new
    o = acc[:, :hd] / acc[:, hd:hd + 1]
    for j in range(r):
        o_ref[:, j * hd:(j + 1) * hd] = o[j * tq:(j + 1) * tq].astype(BF16)


def _attn_a(qa, ka, va, bsz, seq, tq):
    t = qa.shape[1]
    r = A_HEADS // A_KV
    nq = seq // tq
    return pl.pallas_call(
        functools.partial(_attn_a_kernel, kc=min(512, seq)),
        grid=(bsz, A_KV, nq),
        in_specs=[
            pl.BlockSpec((r, tq, HEAD_DIM), lambda b, g, i: (g, b * nq + i, 0)),
            pl.BlockSpec((None, seq, HEAD_DIM), lambda b, g, i: (g, b, 0)),
            pl.BlockSpec((None, seq, 2 * HEAD_DIM), lambda b, g, i: (g, b, 0)),
        ],
        out_specs=pl.BlockSpec((tq, r * HEAD_DIM), lambda b, g, i: (b * nq + i, g)),
        out_shape=jax.ShapeDtypeStruct((t, A_HEADS * HEAD_DIM), BF16),
        compiler_params=_cparams("parallel", "parallel", "parallel"),
        name="attn_a",
    )(qa, ka, va)


def _attn_b_kernel(relb_ref, sink_ref, q_ref, k_ref, v_ref, bucket_ref, o_ref, bias_ref):
    b = pl.program_id(0)
    n = pl.program_id(1)
    nblk = pl.num_programs(1)
    r = B_HEADS // B_KV
    qb = Q_BLOCK
    kw = 3 * Q_BLOCK

    @pl.when(jnp.logical_and(b == 0, n == 0))
    def _():
        bucket = bucket_ref[...]
        for h in range(B_HEADS):
            acc = jnp.full((qb, kw), NEG, F32)
            for k in range(N_BUCKETS):
                acc = jnp.where(bucket == k, relb_ref[k, h] * LOG2E, acc)
            bias_ref[h * qb:(h + 1) * qb, :] = acc

    nq = q_ref.shape[1] // qb
    kj = lax.broadcasted_iota(jnp.int32, (1, kw), 1)
    for i in range(nq):
        blk = n * nq + i
        kpos = blk * qb - qb + kj
        inside = (kpos >= 0) & (kpos < nblk * nq * qb)
        start = pl.multiple_of(blk * qb, qb)
        for g in range(B_KV):
            kwin = k_ref[g, pl.ds(start, kw), :]
            vwin = v_ref[g, pl.ds(start, kw), :]
            q = q_ref[g * r:(g + 1) * r, i * qb:(i + 1) * qb, :].reshape(r * qb, HEAD_DIM)
            sc = _dot_nt(q, kwin) + bias_ref[g * r * qb:(g + 1) * r * qb, :]
            sc = jnp.where(inside, sc, NEG)
            sink = jnp.concatenate(
                [jnp.full((qb, 1), sink_ref[0, g * r + j] * LOG2E, F32) for j in range(r)], axis=0)
            m = jnp.maximum(jnp.max(sc, axis=-1, keepdims=True), sink)
            pv = _dot(jnp.exp2(sc - m).astype(BF16), vwin)
            o = pv[:, :HEAD_DIM] / (pv[:, HEAD_DIM:HEAD_DIM + 1] + jnp.exp2(sink - m))
            for j in range(r):
                h = g * r + j
                o_ref[i * qb:(i + 1) * qb, h * HEAD_DIM:(h + 1) * HEAD_DIM] = o[j * qb:(j + 1) * qb].astype(BF16)


def _attn_b(qb, kb_pad, vb_pad, bucket, rel_bias, sink, bsz, seq):
    t = qb.shape[1]
    nb = seq // Q_BLOCK
    sp = seq + 2 * Q_BLOCK
    nq = 2 if nb % 2 == 0 else 1
    ns = nb // nq
    return pl.pallas_call(
        _attn_b_kernel,
        grid=(bsz, ns),
        in_specs=[
            pl.BlockSpec(memory_space=pltpu.SMEM),
            pl.BlockSpec(memory_space=pltpu.SMEM),
            pl.BlockSpec((B_HEADS, nq * Q_BLOCK, HEAD_DIM), lambda b, n: (0, b * ns + n, 0)),
            pl.BlockSpec((B_KV, None, sp, HEAD_DIM), lambda b, n: (0, b, 0, 0)),
            pl.BlockSpec((B_KV, None, sp, 2 * HEAD_DIM), lambda b, n: (0, b, 0, 0)),
            pl.BlockSpec((Q_BLOCK, 3 * Q_BLOCK), lambda b, n: (0, 0)),
        ],
        out_specs=pl.BlockSpec((nq * Q_BLOCK, B_HEADS * HEAD_DIM), lambda b, n: (b * ns + n, 0)),
        out_shape=jax.ShapeDtypeStruct((t, B_HEADS * HEAD_DIM), BF16),
        scratch_shapes=[pltpu.VMEM((B_HEADS * Q_BLOCK, 3 * Q_BLOCK), F32)],
        compiler_params=_cparams("arbitrary", "arbitrary"),
        name="attn_b",
    )(rel_bias, sink, qb, kb_pad, vb_pad, bucket)


def _merge_kernel(oa_ref, ob_ref, gate_ref, x_ref, wa_ref, wb_ref, wo_ref, gffn_ref, wq_ref, sk3_ref,
                  h_ref, xnt_ref, sc_ref):
    d = x_ref.shape[1]
    pa = _dot(oa_ref[...], wa_ref[...])
    pb = _dot(ob_ref[...], wb_ref[...])
    mix = gate_ref[:, :d].astype(F32) * pa + gate_ref[:, d:].astype(F32) * pb
    h = x_ref[...] + _dot(mix.astype(BF16), wo_ref[...])
    h_ref[...] = h
    xn = h * lax.rsqrt(jnp.mean(h * h, axis=-1, keepdims=True) + EPS) * gffn_ref[...]
    xnb = xn.astype(BF16)
    xnt_ref[...] = xn.T.astype(BF16)
    half = PEER_DQ // 2
    qh, ql = _split_bf16(_dot(xnb, wq_ref[...]))
    for hc in range(2 * PEER_HEADS):
        cols = slice(hc * half, (hc + 1) * half)
        q3 = jnp.concatenate([qh[:, cols], ql[:, cols], qh[:, cols]], axis=1)
        sc_ref[hc] = _dot_nt(sk3_ref[hc], q3)


def _merge(oa, ob, gate, x2, wa, wb, wo, gffn, wq, sk3, tm):
    t, d = x2.shape
    const2 = lambda i: (0, 0)
    const3 = lambda i: (0, 0, 0)
    row = lambda w: pl.BlockSpec((tm, w), lambda i: (i, 0))
    nsc = 2 * PEER_HEADS
    return pl.pallas_call(
        _merge_kernel,
        grid=(t // tm,),
        in_specs=[
            row(oa.shape[1]), row(ob.shape[1]), row(2 * d), row(d),
            pl.BlockSpec(wa.shape, const2), pl.BlockSpec(wb.shape, const2), pl.BlockSpec(wo.shape, const2),
            pl.BlockSpec((1, d), const2), pl.BlockSpec(wq.shape, const2),
            pl.BlockSpec(sk3.shape, const3),
        ],
        out_specs=[row(d), pl.BlockSpec((d, tm), lambda i: (0, i)),
                   pl.BlockSpec((nsc, PEER_NKEYS, tm), lambda i: (0, 0, i))],
        out_shape=[
            jax.ShapeDtypeStruct((t, d), F32),
            jax.ShapeDtypeStruct((d, t), BF16),
            jax.ShapeDtypeStruct((nsc, PEER_NKEYS, t), F32),
        ],
        compiler_params=_cparams("parallel"),
        name="merge",
    )(oa, ob, gate, x2, wa, wb, wo, gffn, wq, sk3)


def _pop_max(x, order, exact):
    m = jnp.max(x, axis=0, keepdims=True)
    if exact:
        first = jnp.min(jnp.where(x == m, order, jnp.inf), axis=0, keepdims=True)
        return m, order == first
    return m, x == m


def _topk_ranks(s):
    rows = lax.broadcasted_iota(jnp.int32, s.shape, 0).astype(F32)
    rank = jnp.full(s.shape, float(PEER_TOPK), F32)
    x = s
    vals = []
    for a in range(PEER_TOPK):
        m, hit = _pop_max(x, rows, True)
        rank = jnp.where(hit, float(a), rank)
        x = jnp.where(hit, -jnp.inf, x)
        vals.append(m)
    return vals, rank


def _batcher_layers(n):
    layers = []
    p = 1
    while p < n:
        k = p
        while k >= 1:
            layer = []
            for j in range(k % p, n - k, 2 * k):
                for i in range(min(k, n - j - k)):
                    if (i + j) // (2 * p) == (i + j + k) // (2 * p):
                        layer.append((i + j, i + j + k))
            layers.append(layer)
            k //= 2
        p *= 2
    return layers


_SORT16 = _batcher_layers(PEER_TOPK)


def _exchange(xs, i, j):
    xs[i], xs[j] = jnp.maximum(xs[i], xs[j]), jnp.minimum(xs[i], xs[j])


def _sorted_top16(xs):
    xs = list(xs)
    n = len(xs)
    for layer in _SORT16:
        for i, j in layer:
            _exchange(xs, i, j)
    for shift in (4, 2, 1):
        other = [pltpu.roll(x, shift, 0) for x in xs]
        xs = [jnp.maximum(xs[p], other[n - 1 - p]) for p in range(n)]
        stride = n // 2
        while stride >= 1:
            for base in range(0, n, 2 * stride):
                for i in range(base, base + stride):
                    _exchange(xs, i, i + stride)
            stride //= 2
    return xs


def _select_kernel(sc_ref, flat_ref, arow_ref, rowsel_ref, c1_ref, lim_ref, p2_ref, r2_ref, cand_ref):
    tb = sc_ref.shape[2]
    flat = flat_ref[...]
    arow = arow_ref[...]
    cand_ref[...] = jnp.full(cand_ref.shape, -jnp.inf, F32)
    nv = PEER_NKEYS // 8
    k16 = float(PEER_TOPK)

    def head_fast(h, cand):
        s1 = [sc_ref[2 * h, 8 * k:8 * k + 8, :] for k in range(nv)]
        s2 = [sc_ref[2 * h + 1, 8 * k:8 * k + 8, :] for k in range(nv)]
        v1 = _sorted_top16(s1)
        v2 = _sorted_top16(s2)
        r1 = [v[0:1] for v in v1]
        r2 = [v[0:1] for v in v2]
        for i, (a, b) in enumerate(_CAND):
            cand[i:i + 1, :] = r1[a] + r2[b]
        top = r1[0] + r2[0]
        sums = cand[...]
        x = sums
        for _ in range(PEER_TOPK):
            m, hit = _pop_max(x, flat, False)
            x = jnp.where(hit, -jnp.inf, x)
        sel = sums >= m
        z = jnp.sum(jnp.where(sel, jnp.exp(sums - top), 0.0), axis=0, keepdims=True)
        counts = _dot(rowsel_ref[...], jnp.where(sel, 1.0, 0.0).astype(BF16))
        bmax = [jnp.broadcast_to(counts[a:a + 1], (8, tb)) for a in range(PEER_TOPK)]
        lim, rank2 = [], []
        for k in range(nv):
            l = jnp.zeros((8, tb), F32)
            r = jnp.full((8, tb), k16, F32)
            for a in range(PEER_TOPK - 1, -1, -1):
                l = jnp.where(s1[k] >= v1[a], bmax[a], l)
                r = jnp.where(s2[k] >= v2[a], float(a), r)
            lim.append(l)
            rank2.append(r)
        c1_ref[h] = 0.5 * jnp.exp(jnp.concatenate(s1, axis=0) - r1[0]) / z
        lim_ref[h] = jnp.concatenate(lim, axis=0)
        p2_ref[h] = jnp.exp(jnp.concatenate(s2, axis=0) - r2[0]).astype(BF16)
        r2_ref[h] = jnp.concatenate(rank2, axis=0).astype(BF16)
        tied = jnp.sum(counts, axis=0, keepdims=True) != k16
        for s, v, r in ((s1, v1, r1), (s2, v2, r2)):
            count = sum(jnp.where(x >= v[-1], 1.0, 0.0) for x in s)
            tied = tied | (jnp.sum(count, axis=0, keepdims=True) != k16)
            for a in range(PEER_TOPK - 1):
                tied = tied | (r[a] == r[a + 1])
        return tied

    def head_exact(h, cand):
        s1 = sc_ref[2 * h]
        s2 = sc_ref[2 * h + 1]
        v1, rank1 = _topk_ranks(s1)
        v2, rank2 = _topk_ranks(s2)
        for i, (a, b) in enumerate(_CAND):
            cand[i:i + 1, :] = v1[a] + v2[b]
        top = v1[0] + v2[0]
        sel = jnp.zeros(cand.shape, F32)
        z = jnp.zeros((1, tb), F32)
        x = cand[...]
        for _ in range(PEER_TOPK):
            m, hit = _pop_max(x, flat, True)
            sel = jnp.where(hit, 1.0, sel)
            x = jnp.where(hit, -jnp.inf, x)
            z = z + jnp.exp(m - top)
        lim = jnp.zeros(s1.shape, F32)
        for a in range(PEER_TOPK):
            bmax = jnp.sum(jnp.where(arow == float(a), sel, 0.0), axis=0, keepdims=True)
            lim = jnp.where(rank1 == float(a), bmax, lim)
        c1_ref[h] = 0.5 * jnp.exp(s1 - v1[0]) / z
        lim_ref[h] = lim
        p2_ref[h] = jnp.exp(s2 - v2[0]).astype(BF16)
        r2_ref[h] = rank2.astype(BF16)

    group = cand_ref.shape[0]

    def head_group(i, carry):
        tied = [head_fast(group * i + j, cand_ref.at[j]) for j in range(group)]
        for j in range(group):
            @pl.when(jnp.max(jnp.where(tied[j], 1.0, 0.0)) > 0.0)
            def _():
                head_exact(group * i + j, cand_ref.at[j])

        return carry

    lax.fori_loop(0, PEER_HEADS // group, head_group, 0)


def _select(sc, tb):
    nsc, nk, t = sc.shape
    flat = np.full((_NCAND_PAD, 1), 1023.0, np.float32)
    arow = np.full((_NCAND_PAD, 1), -1.0, np.float32)
    for i, (a, b) in enumerate(_CAND):
        flat[i, 0] = a * PEER_TOPK + b
        arow[i, 0] = a
    rowsel = jnp.asarray(arow.T == np.arange(PEER_TOPK, dtype=np.float32)[:, None]).astype(BF16)
    flat = jnp.asarray(np.broadcast_to(flat, (_NCAND_PAD, tb)))
    arow = jnp.asarray(np.broadcast_to(arow, (_NCAND_PAD, tb)))
    out = lambda dt: jax.ShapeDtypeStruct((PEER_HEADS, nk, t), dt)
    ospec = pl.BlockSpec((PEER_HEADS, nk, tb), lambda i: (0, 0, i))
    cspec = pl.BlockSpec((_NCAND_PAD, tb), lambda i: (0, 0))
    return pl.pallas_call(
        _select_kernel,
        grid=(t // tb,),
        in_specs=[pl.BlockSpec((nsc, nk, tb), lambda i: (0, 0, i)), cspec, cspec,
                  pl.BlockSpec((PEER_TOPK, _NCAND_PAD), lambda i: (0, 0))],
        out_specs=[ospec, ospec, ospec, ospec],
        out_shape=[out(F32), out(F32), out(BF16), out(BF16)],
        scratch_shapes=[pltpu.VMEM((2, _NCAND_PAD, tb), F32)],
        compiler_params=_cparams("parallel"),
        name="select",
    )(sc, flat, arow, rowsel)


def _experts_kernel(xnt_ref, u_ref, vt_ref, c1_ref, lim_ref, p2_ref, r2_ref, h_ref, gfin_ref, o_ref,
                    acc_ref, g_ref, w_ref, *, nblk):
    s = pl.program_id(0)
    e = lax.rem(s, nblk)
    rows = c1_ref.shape[1]
    nk = PEER_NKEYS

    slot = lax.rem(s, 2)
    g_new = g_ref.at[slot]
    g_old = g_ref.at[1 - slot]

    @pl.when(s == 0)
    def _():
        acc_ref[...] = jnp.zeros_like(acc_ref)
        g_old[...] = jnp.zeros_like(g_old)

    def row_bf16(ref, h, r):
        x = jnp.broadcast_to(ref[h, r:r + 1, :], (16, ref.shape[2])).astype(BF16)
        return jnp.concatenate([x] * (nk // 16), axis=0)

    for r in range(rows):
        w = None
        for h in range(PEER_HEADS):
            term = jnp.where(r2_ref[h] < row_bf16(lim_ref, h, r), p2_ref[h], jnp.zeros((), BF16)) \
                * row_bf16(c1_ref, h, r)
            w = term if w is None else w + term
        w_ref[r * nk:(r + 1) * nk, :] = w

    mc = min(512, u_ref.shape[0])
    a = jnp.concatenate([_dot(u_ref[i:i + mc, :], xnt_ref[...]) for i in range(0, u_ref.shape[0], mc)], axis=0)
    contrib = _dot(vt_ref[...], g_old[...])
    acc_ref[...] = jnp.where(lax.rem(s + nblk - 1, nblk) == 0, contrib, acc_ref[...] + contrib)
    g_new[...] = (a * (1.0 + lax.erf(a * INV_SQRT2))).astype(BF16) * w_ref[...]

    @pl.when(jnp.logical_and(e == 0, s > 0))
    def _():
        hh = h_ref[...] + acc_ref[...].T
        o_ref[...] = hh * lax.rsqrt(jnp.mean(hh * hh, axis=-1, keepdims=True) + EPS) * gfin_ref[...]


def _experts(xnt, u_bf, vt_blk, c1, lim, p2, r2, h, gfin, tt):
    t, d = h.shape
    nblk, _, et = vt_blk.shape
    rows = et // PEER_NKEYS
    nt = t // tt
    tile = lambda s: jnp.minimum(s // nblk, nt - 1)
    blk = lambda s: lax.rem(s, nblk)
    prev_tile = lambda s: jnp.maximum(s - 1, 0) // nblk
    prev_blk = lambda s: lax.rem(s + nblk - 1, nblk)
    tok3 = pl.BlockSpec((PEER_HEADS, PEER_NKEYS, tt), lambda s: (0, 0, tile(s)))
    row3 = pl.BlockSpec((PEER_HEADS, rows, tt), lambda s: (0, blk(s), tile(s)))
    return pl.pallas_call(
        functools.partial(_experts_kernel, nblk=nblk),
        grid=(nt * nblk + 1,),
        in_specs=[
            pl.BlockSpec((d, tt), lambda s: (0, tile(s))),
            pl.BlockSpec((et, d), lambda s: (blk(s), 0)),
            pl.BlockSpec((None, d, et), lambda s: (prev_blk(s), 0, 0)),
            row3, row3, tok3, tok3,
            pl.BlockSpec((tt, d), lambda s: (prev_tile(s), 0)),
            pl.BlockSpec((1, d), lambda s: (0, 0)),
        ],
        out_specs=pl.BlockSpec((tt, d), lambda s: (prev_tile(s), 0)),
        out_shape=jax.ShapeDtypeStruct((t, d), F32),
        scratch_shapes=[pltpu.VMEM((d, tt), F32), pltpu.VMEM((2, et, tt), BF16), pltpu.VMEM((et, tt), BF16)],
        compiler_params=_cparams("arbitrary"),
        name="experts",
    )(xnt, u_bf, vt_blk, c1, lim, p2, r2, h, gfin)


def _rope_tables(seq):
    rows = seq // GRID_W
    row = np.broadcast_to(np.arange(rows)[:, None], (rows, GRID_W)).reshape(-1)
    col = np.broadcast_to(np.arange(GRID_W)[None, :], (rows, GRID_W)).reshape(-1)
    half = HEAD_DIM // 4
    freqs = ROPE_THETA ** (-np.arange(half, dtype=np.float64) / half)

    def cs(pos):
        ang = pos.astype(np.float64)[:, None] * freqs[None, :]
        return np.cos(ang), np.sin(ang)

    cr, sr = cs(row)
    cc, sc = cs(col)
    zero = np.zeros_like(sr)
    cos = np.concatenate([cr, cr, cc, cc], axis=1)
    sa = np.concatenate([-sr, zero, -sc, zero], axis=1)
    sb = np.concatenate([zero, sr, zero, sc], axis=1)
    tile = lambda a: jnp.asarray(np.concatenate([a, a], axis=1).astype(np.float32))
    return tile(cos), tile(sa), tile(sb)


def _bucket_table():
    qi = np.arange(Q_BLOCK)
    kj = np.arange(3 * Q_BLOCK)
    rel = kj[None, :] - Q_BLOCK - qi[:, None]
    nbk = N_BUCKETS // 2
    max_exact = nbk // 2
    assert (max_exact, MAX_DISTANCE, nbk - max_exact) == (8, 128, 8), "the integer form below assumes these constants"
    n = np.abs(rel)
    log_part = np.floor(np.log2(np.maximum(n * n // 64, 1))).astype(np.int64)
    log_part = np.where(2 ** (log_part + 1) * 64 <= n * n, log_part + 1, log_part)
    log_part = np.where(2 ** log_part * 64 > n * n, log_part - 1, log_part)
    large = np.minimum(max_exact + log_part, nbk - 1)
    bucket = np.where(rel > 0, nbk, 0) + np.where(n < max_exact, n, large)
    return jnp.asarray(np.where(n <= WINDOW, bucket, N_BUCKETS).astype(np.int32))


def _tile_sizes(seq, tokens):
    tm = min(512, seq)
    tq = min(512, seq)
    tb = min(128, tokens)
    tt = min(512, tokens)
    return tm, tq, tb, tt


def kernel(x, g_mix, w_in, b_gate, q_norm_g, k_norm_g, rel_bias, sink, w_branch_a, w_branch_b, w_out, g_ffn,
           peer_wq, peer_subkeys, peer_u, peer_v, g_final):
    bsz, seq, d = x.shape
    t = bsz * seq
    depth = w_in.shape[0]
    tm, tq, tb, tt = _tile_sizes(seq, t)
    assert depth == 1, "the experts stage fuses the final norm, so only a single layer is supported"
    assert seq % tm == 0 and seq % Q_BLOCK == 0 and t % tb == 0 and t % tt == 0

    cos, sa, sb = _rope_tables(seq)
    bucket = _bucket_table()
    bd = jnp.asarray(np.kron(np.eye(2, dtype=np.float32),
                             np.full((HEAD_DIM, HEAD_DIM), 1.0 / HEAD_DIM, np.float32))).astype(BF16)
    expert_block = 2048

    h = x.reshape(t, d)
    for l in range(depth):
        qg = jnp.tile(q_norm_g[l], 2)[None, :]
        kg = jnp.tile(k_norm_g[l], 2)[None, :]
        qa, ka, va, qb, kb, vb, gate = _in_proj(
            h, g_mix[l][None, :], w_in[l].astype(BF16), b_gate[l][None, :], qg, kg, cos, sa, sb, bd, seq, tm)
        oa = _attn_a(qa, ka, va, bsz, seq, tq)
        pad = lambda a: jnp.pad(a.reshape(B_KV, bsz, seq, a.shape[-1]), ((0, 0), (0, 0), (Q_BLOCK, Q_BLOCK), (0, 0)))
        ob = _attn_b(qb, pad(kb), pad(vb), bucket, rel_bias, sink[l][None, :], bsz, seq)
        skh, skl = _split_bf16(peer_subkeys[l].reshape(2 * PEER_HEADS, PEER_NKEYS, PEER_DQ // 2))
        sk3 = jnp.concatenate([skh, skh, skl], axis=-1)
        h1, xnt, sc = _merge(oa, ob, gate, h, w_branch_a[l].astype(BF16), w_branch_b[l].astype(BF16),
                            w_out[l].astype(BF16), g_ffn[l][None, :], peer_wq[l].astype(BF16), sk3, tm)
        c1, lim, p2, r2 = _select(sc, tb)
        vt_blk = peer_v[l].astype(BF16).reshape(-1, expert_block, d).transpose(0, 2, 1)
        out = _experts(xnt, peer_u[l].astype(BF16), vt_blk, c1, lim, p2, r2, h1, g_final[None, :], tt)
    return out.reshape(bsz, seq, d)
```

```python
import functools
import math

import numpy as np
import jax
import jax.numpy as jnp
from jax import lax
from jax.experimental import pallas as pl
from jax.experimental.pallas import tpu as pltpu

F32 = jnp.float32
BF16 = jnp.bfloat16

HEAD_DIM = 64
A_HEADS, A_KV = 8, 2
B_HEADS, B_KV = 8, 2
Q_BLOCK = 128
WINDOW = 128
GRID_W = 64
ROPE_THETA = 10000.0
N_BUCKETS = 32
MAX_DISTANCE = 128
PEER_HEADS = 8
PEER_NKEYS = 128
PEER_TOPK = 16
PEER_DQ = 256
EPS = 1e-6
NEG = -1e30
INV_SQRT2 = 0.7071067811865476
LOG2E = 1.4426950408889634

V7X_LANES = 128
V7X_VMEM_LIMIT_BYTES = 56 * 1024 * 1024

_CAND = [(a, b) for a in range(PEER_TOPK) for b in range(PEER_TOPK) if (a + 1) * (b + 1) <= PEER_TOPK]
_NCAND = len(_CAND)
_NCAND_PAD = -(-_NCAND // 8) * 8


def _cparams(*sem):
    return pltpu.CompilerParams(dimension_semantics=sem, vmem_limit_bytes=V7X_VMEM_LIMIT_BYTES)


def _split_bf16(x):
    hi = x.astype(BF16)
    lo = (x - hi.astype(F32)).astype(BF16)
    return hi, lo


def _dot(a, b):
    return jnp.dot(a, b, preferred_element_type=F32)


def _dot_nt(a, b):
    return lax.dot_general(a, b, (((1,), (1,)), ((), ())), preferred_element_type=F32)


def _in_proj_kernel(x_ref, gmix_ref, w_ref, bg_ref, qg_ref, kg_ref, cos_ref, sa_ref, sb_ref, bd_ref,
                    qa_ref, ka_ref, va_ref, qbt_ref, kb_ref, vbt_ref, gate_ref):
    d = x_ref.shape[1]
    x = x_ref[...]
    n = x * lax.rsqrt(jnp.mean(x * x, axis=-1, keepdims=True) + EPS) * gmix_ref[...]
    nb = n.astype(BF16)
    cos, sa, sb, bd = cos_ref[...], sa_ref[...], sb_ref[...], bd_ref[...]

    def norm_rope(q, gain):
        hi, lo = _split_bf16(q * q)
        ms = _dot(hi, bd) + _dot(lo, bd)
        y = q * lax.rsqrt(ms + EPS) * gain
        return y * cos + pltpu.roll(y, V7X_LANES - 16, 1) * sa + pltpu.roll(y, 16, 1) * sb

    def put_heads(ref, first, y):
        ref[first] = y[:, :HEAD_DIM].astype(BF16)
        ref[first + 1] = y[:, HEAD_DIM:].astype(BF16)

    def put_values(ref, y):
        lane = lax.broadcasted_iota(jnp.int32, y.shape, 1)
        tail = jnp.where(lane == HEAD_DIM, 1.0, 0.0)
        ref[0] = jnp.where(lane < HEAD_DIM, y, tail).astype(BF16)
        ref[1] = jnp.where(lane < HEAD_DIM, pltpu.roll(y, HEAD_DIM, 1), tail).astype(BF16)

    qscale = HEAD_DIM ** -0.5 * LOG2E
    n_attn = (A_HEADS + 2 * A_KV + B_HEADS + 2 * B_KV) * HEAD_DIM
    z = _dot(nb, w_ref[:, :n_attn])
    col = lambda c: z[:, c * 128:(c + 1) * 128]
    c = 0
    for j in range(A_HEADS // 2):
        put_heads(qa_ref, 2 * j, norm_rope(col(c), qg_ref[...]) * qscale)
        c += 1
    put_heads(ka_ref, 0, norm_rope(col(c), kg_ref[...]))
    put_values(va_ref, col(c + 1))
    c += 2
    nqb = B_HEADS // 2
    qbt_ref[...] = (z[:, c * 128:(c + nqb) * 128] * qscale).T.astype(BF16)
    c += nqb
    put_heads(kb_ref, 0, col(c))
    y = col(c + 1)
    lane = lax.broadcasted_iota(jnp.int32, y.shape, 1)
    tail = jnp.where(lane == HEAD_DIM, 1.0, 0.0)
    vbt_ref[0] = jnp.where(lane < HEAD_DIM, y, tail).T.astype(BF16)
    vbt_ref[1] = jnp.where(lane < HEAD_DIM, pltpu.roll(y, HEAD_DIM, 1), tail).T.astype(BF16)
    for j in range(2 * d // 512):
        lo, hi = n_attn + j * 512, n_attn + (j + 1) * 512
        zg = _dot(nb, w_ref[:, lo:hi]) + bg_ref[:, j * 512:(j + 1) * 512]
        gate_ref[:, j * 512:(j + 1) * 512] = (1.0 / (1.0 + jnp.exp(-zg))).astype(BF16)


def _in_proj(x2, gmix, w_bf, bg, qg, kg, cos, sa, sb, bd, seq, tm):
    t, d = x2.shape
    nt = t // tm
    ns = seq // tm
    const = lambda i: (0, 0)
    tab = pl.BlockSpec((tm, 128), lambda i: (i % ns, 0))
    head = lambda nh: pl.BlockSpec((nh, tm, HEAD_DIM), lambda i: (0, i, 0))
    wide = lambda nh: pl.BlockSpec((nh, tm, 2 * HEAD_DIM), lambda i: (0, i, 0))
    return pl.pallas_call(
        _in_proj_kernel,
        grid=(nt,),
        in_specs=[
            pl.BlockSpec((tm, d), lambda i: (i, 0)),
            pl.BlockSpec((1, d), const),
            pl.BlockSpec(w_bf.shape, const),
            pl.BlockSpec((1, 2 * d), const),
            pl.BlockSpec((1, 128), const),
            pl.BlockSpec((1, 128), const),
            tab, tab, tab,
            pl.BlockSpec((128, 128), const),
        ],
        out_specs=[head(A_HEADS), head(A_KV), wide(A_KV),
                   pl.BlockSpec((B_HEADS * HEAD_DIM, tm), lambda i: (0, i)), head(B_KV),
                   pl.BlockSpec((B_KV, 2 * HEAD_DIM, tm), lambda i: (0, 0, i)),
                   pl.BlockSpec((tm, 2 * d), lambda i: (i, 0))],
        out_shape=[
            jax.ShapeDtypeStruct((A_HEADS, t, HEAD_DIM), BF16),
            jax.ShapeDtypeStruct((A_KV, t, HEAD_DIM), BF16),
            jax.ShapeDtypeStruct((A_KV, t, 2 * HEAD_DIM), BF16),
            jax.ShapeDtypeStruct((B_HEADS * HEAD_DIM, t), BF16),
            jax.ShapeDtypeStruct((B_KV, t, HEAD_DIM), BF16),
            jax.ShapeDtypeStruct((B_KV, 2 * HEAD_DIM, t), BF16),
            jax.ShapeDtypeStruct((t, 2 * d), BF16),
        ],
        compiler_params=_cparams("parallel"),
        name="in_proj",
    )(x2, gmix, w_bf, bg, qg, kg, cos, sa, sb, bd)


def _attn_a_kernel(q_ref, k_ref, v_ref, o_ref, *, kc):
    r, tq, hd = q_ref.shape
    seq = k_ref.shape[0]
    q = q_ref[...].reshape(r * tq, hd)
    m = jnp.full((r * tq, 1), -jnp.inf, F32)
    acc = jnp.zeros((r * tq, v_ref.shape[1]), F32)
    for c in range(seq // kc):
        s = _dot_nt(q, k_ref[c * kc:(c + 1) * kc, :])
        m_new = jnp.maximum(m, jnp.max(s, axis=-1, keepdims=True))
        p = jnp.exp2(s - m_new)
        acc = jnp.exp2(m - m_new) * acc + _dot(p.astype(BF16), v_ref[c * kc:(c + 1) * kc, :])
        m = m_new
    o = acc[:, :hd] / acc[:, hd:hd + 1]
    for j in range(r):
        o_ref[:, j * hd:(j + 1) * hd] = o[j * tq:(j + 1) * tq].astype(BF16)


def _attn_a(qa, ka, va, bsz, seq, tq):
    t = qa.shape[1]
    r = A_HEADS // A_KV
    nq = seq // tq
    return pl.pallas_call(
        functools.partial(_attn_a_kernel, kc=min(512, seq)),
        grid=(bsz, A_KV, nq),
        in_specs=[
            pl.BlockSpec((r, tq, HEAD_DIM), lambda b, g, i: (g, b * nq + i, 0)),
            pl.BlockSpec((None, seq, HEAD_DIM), lambda b, g, i: (g, b, 0)),
            pl.BlockSpec((None, seq, 2 * HEAD_DIM), lambda b, g, i: (g, b, 0)),
        ],
        out_specs=pl.BlockSpec((tq, r * HEAD_DIM), lambda b, g, i: (b * nq + i, g)),
        out_shape=jax.ShapeDtypeStruct((t, A_HEADS * HEAD_DIM), BF16),
        compiler_params=_cparams("parallel", "parallel", "parallel"),
        name="attn_a",
    )(qa, ka, va)


def _attn_b_kernel(relb_ref, sink_ref, qt_ref, k_ref, vt_ref, bucket_t_ref, o_ref, bias_ref):
    b = pl.program_id(0)
    n = pl.program_id(1)
    nblk = pl.num_programs(1)
    r = B_HEADS // B_KV
    qb = Q_BLOCK
    kw = 3 * Q_BLOCK
    hd = HEAD_DIM

    @pl.when(jnp.logical_and(b == 0, n == 0))
    def _():
        bucket_t = bucket_t_ref[...]
        for h in range(B_HEADS):
            acc = jnp.full((kw, qb), NEG, F32)
            for k in range(N_BUCKETS):
                acc = jnp.where(bucket_t == k, relb_ref[k, h] * LOG2E, acc)
            bias_ref[h // r, :, (h % r) * qb:(h % r + 1) * qb] = acc

    nq = qt_ref.shape[1] // qb
    kidx = lax.broadcasted_iota(jnp.int32, (kw, r * qb), 0)
    head_of_lane = lax.broadcasted_iota(jnp.int32, (1, r * qb), 1) // qb
    for i in range(nq):
        blk = n * nq + i
        kpos = blk * qb - qb + kidx
        inside = (kpos >= 0) & (kpos < nblk * nq * qb)
        start = pl.multiple_of(blk * qb, qb)
        for g in range(B_KV):
            kwin = k_ref[g, pl.ds(start, kw), :]
            vtw = vt_ref[g, :, pl.ds(start, kw)]
            qt = jnp.concatenate(
                [qt_ref[(g * r + j) * hd:(g * r + j + 1) * hd, i * qb:(i + 1) * qb] for j in range(r)], axis=1)
            st = jnp.where(inside, _dot(kwin, qt) + bias_ref[g], NEG)
            sink = jnp.zeros((1, r * qb), F32)
            for j in range(r):
                sink = jnp.where(head_of_lane == j, sink_ref[0, g * r + j] * LOG2E, sink)
            m = jnp.maximum(jnp.max(st, axis=0, keepdims=True), sink)
            ot = _dot(vtw, jnp.exp2(st - m).astype(BF16))
            ot = ot / (ot[hd:hd + 1] + jnp.exp2(sink - m))
            for j in range(r):
                h = g * r + j
                o_ref[i * qb:(i + 1) * qb, h * hd:(h + 1) * hd] = ot[:, j * qb:(j + 1) * qb].T[:, :hd].astype(BF16)


def _attn_b(qbt, kb_pad, vbt_pad, bucket_t, rel_bias, sink, bsz, seq):
    t = qbt.shape[1]
    nb = seq // Q_BLOCK
    sp = seq + 2 * Q_BLOCK
    nq = 2 if nb % 2 == 0 else 1
    ns = nb // nq
    r = B_HEADS // B_KV
    return pl.pallas_call(
        _attn_b_kernel,
        grid=(bsz, ns),
        in_specs=[
            pl.BlockSpec(memory_space=pltpu.SMEM),
            pl.BlockSpec(memory_space=pltpu.SMEM),
            pl.BlockSpec((B_HEADS * HEAD_DIM, nq * Q_BLOCK), lambda b, n: (0, b * ns + n)),
            pl.BlockSpec((B_KV, None, sp, HEAD_DIM), lambda b, n: (0, b, 0, 0)),
            pl.BlockSpec((B_KV, None, 2 * HEAD_DIM, sp), lambda b, n: (0, b, 0, 0)),
            pl.BlockSpec((3 * Q_BLOCK, Q_BLOCK), lambda b, n: (0, 0)),
        ],
        out_specs=pl.BlockSpec((nq * Q_BLOCK, B_HEADS * HEAD_DIM), lambda b, n: (b * ns + n, 0)),
        out_shape=jax.ShapeDtypeStruct((t, B_HEADS * HEAD_DIM), BF16),
        scratch_shapes=[pltpu.VMEM((B_KV, 3 * Q_BLOCK, r * Q_BLOCK), F32)],
        compiler_params=_cparams("arbitrary", "arbitrary"),
        name="attn_b",
    )(rel_bias, sink, qbt, kb_pad, vbt_pad, bucket_t)


def _merge_kernel(oa_ref, ob_ref, gate_ref, x_ref, wa_ref, wb_ref, wo_ref, gffn_ref, wq_ref, sk3_ref,
                  h_ref, xnt_ref, sc_ref):
    d = x_ref.shape[1]
    pa = _dot(oa_ref[...], wa_ref[...])
    pb = _dot(ob_ref[...], wb_ref[...])
    mix = gate_ref[:, :d].astype(F32) * pa + gate_ref[:, d:].astype(F32) * pb
    h = x_ref[...] + _dot(mix.astype(BF16), wo_ref[...])
    h_ref[...] = h
    xn = h * lax.rsqrt(jnp.mean(h * h, axis=-1, keepdims=True) + EPS) * gffn_ref[...]
    xnb = xn.astype(BF16)
    xnt_ref[...] = xn.T.astype(BF16)
    half = PEER_DQ // 2
    qh, ql = _split_bf16(_dot(xnb, wq_ref[...]))
    for hc in range(2 * PEER_HEADS):
        cols = slice(hc * half, (hc + 1) * half)
        q3 = jnp.concatenate([qh[:, cols], ql[:, cols], qh[:, cols]], axis=1)
        sc_ref[hc] = _dot_nt(sk3_ref[hc], q3)


def _merge(oa, ob, gate, x2, wa, wb, wo, gffn, wq, sk3, tm):
    t, d = x2.shape
    const2 = lambda i: (0, 0)
    const3 = lambda i: (0, 0, 0)
    row = lambda w: pl.BlockSpec((tm, w), lambda i: (i, 0))
    nsc = 2 * PEER_HEADS
    return pl.pallas_call(
        _merge_kernel,
        grid=(t // tm,),
        in_specs=[
            row(oa.shape[1]), row(ob.shape[1]), row(2 * d), row(d),
            pl.BlockSpec(wa.shape, const2), pl.BlockSpec(wb.shape, const2), pl.BlockSpec(wo.shape, const2),
            pl.BlockSpec((1, d), const2), pl.BlockSpec(wq.shape, const2),
            pl.BlockSpec(sk3.shape, const3),
        ],
        out_specs=[row(d), pl.BlockSpec((d, tm), lambda i: (0, i)),
                   pl.BlockSpec((nsc, PEER_NKEYS, tm), lambda i: (0, 0, i))],
        out_shape=[
            jax.ShapeDtypeStruct((t, d), F32),
            jax.ShapeDtypeStruct((d, t), BF16),
            jax.ShapeDtypeStruct((nsc, PEER_NKEYS, t), F32),
        ],
        compiler_params=_cparams("parallel"),
        name="merge",
    )(oa, ob, gate, x2, wa, wb, wo, gffn, wq, sk3)


def _pop_max(x, order, exact):
    m = jnp.max(x, axis=0, keepdims=True)
    if exact:
        first = jnp.min(jnp.where(x == m, order, jnp.inf), axis=0, keepdims=True)
        return m, order == first
    return m, x == m


def _topk_ranks(s):
    rows = lax.broadcasted_iota(jnp.int32, s.shape, 0).astype(F32)
    rank = jnp.full(s.shape, float(PEER_TOPK), F32)
    x = s
    vals = []
    for a in range(PEER_TOPK):
        m, hit = _pop_max(x, rows, True)
        rank = jnp.where(hit, float(a), rank)
        x = jnp.where(hit, -jnp.inf, x)
        vals.append(m)
    return vals, rank


def _batcher_layers(n):
    layers = []
    p = 1
    while p < n:
        k = p
        while k >= 1:
            layer = []
            for j in range(k % p, n - k, 2 * k):
                for i in range(min(k, n - j - k)):
                    if (i + j) // (2 * p) == (i + j + k) // (2 * p):
                        layer.append((i + j, i + j + k))
            layers.append(layer)
            k //= 2
        p *= 2
    return layers


_SORT16 = _batcher_layers(PEER_TOPK)


def _exchange(xs, i, j):
    xs[i], xs[j] = jnp.maximum(xs[i], xs[j]), jnp.minimum(xs[i], xs[j])


def _sorted_top16(xs):
    xs = list(xs)
    n = len(xs)
    for layer in _SORT16:
        for i, j in layer:
            _exchange(xs, i, j)
    for shift in (4, 2, 1):
        other = [pltpu.roll(x, shift, 0) for x in xs]
        xs = [jnp.maximum(xs[p], other[n - 1 - p]) for p in range(n)]
        stride = n // 2
        while stride >= 1:
            for base in range(0, n, 2 * stride):
                for i in range(base, base + stride):
                    _exchange(xs, i, i + stride)
            stride //= 2
    return xs


def _select_kernel(sc_ref, flat_ref, arow_ref, rowsel_ref, c1_ref, lim_ref, p2_ref, r2_ref, cand_ref):
    tb = sc_ref.shape[2]
    flat = flat_ref[...]
    arow = arow_ref[...]
    cand_ref[...] = jnp.full(cand_ref.shape, -jnp.inf, F32)
    nv = PEER_NKEYS // 8
    k16 = float(PEER_TOPK)

    def head_fast(h, cand):
        s1 = [sc_ref[2 * h, 8 * k:8 * k + 8, :] for k in range(nv)]
        s2 = [sc_ref[2 * h + 1, 8 * k:8 * k + 8, :] for k in range(nv)]
        v1 = _sorted_top16(s1)
        v2 = _sorted_top16(s2)
        r1 = [v[0:1] for v in v1]
        r2 = [v[0:1] for v in v2]
        for i, (a, b) in enumerate(_CAND):
            cand[i:i + 1, :] = r1[a] + r2[b]
        top = r1[0] + r2[0]
        sums = cand[...]
        x = sums
        for _ in range(PEER_TOPK):
            m, hit = _pop_max(x, flat, False)
            x = jnp.where(hit, -jnp.inf, x)
        sel = sums >= m
        z = jnp.sum(jnp.where(sel, jnp.exp(sums - top), 0.0), axis=0, keepdims=True)
        counts = _dot(rowsel_ref[...], jnp.where(sel, 1.0, 0.0).astype(BF16))
        bmax = [jnp.broadcast_to(counts[a:a + 1], (8, tb)) for a in range(PEER_TOPK)]
        lim, rank2 = [], []
        for k in range(nv):
            l = jnp.zeros((8, tb), F32)
            r = jnp.full((8, tb), k16, F32)
            for a in range(PEER_TOPK - 1, -1, -1):
                l = jnp.where(s1[k] >= v1[a], bmax[a], l)
                r = jnp.where(s2[k] >= v2[a], float(a), r)
            lim.append(l)
            rank2.append(r)
        c1_ref[h] = 0.5 * jnp.exp(jnp.concatenate(s1, axis=0) - r1[0]) / z
        lim_ref[h] = jnp.concatenate(lim, axis=0)
        p2_ref[h] = jnp.exp(jnp.concatenate(s2, axis=0) - r2[0]).astype(BF16)
        r2_ref[h] = jnp.concatenate(rank2, axis=0).astype(BF16)
        tied = jnp.sum(counts, axis=0, keepdims=True) != k16
        for s, v, r in ((s1, v1, r1), (s2, v2, r2)):
            count = sum(jnp.where(x >= v[-1], 1.0, 0.0) for x in s)
            tied = tied | (jnp.sum(count, axis=0, keepdims=True) != k16)
            for a in range(PEER_TOPK - 1):
                tied = tied | (r[a] == r[a + 1])
        return tied

    def head_exact(h, cand):
        s1 = sc_ref[2 * h]
        s2 = sc_ref[2 * h + 1]
        v1, rank1 = _topk_ranks(s1)
        v2, rank2 = _topk_ranks(s2)
        for i, (a, b) in enumerate(_CAND):
            cand[i:i + 1, :] = v1[a] + v2[b]
        top = v1[0] + v2[0]
        sel = jnp.zeros(cand.shape, F32)
        z = jnp.zeros((1, tb), F32)
        x = cand[...]
        for _ in range(PEER_TOPK):
            m, hit = _pop_max(x, flat, True)
            sel = jnp.where(hit, 1.0, sel)
            x = jnp.where(hit, -jnp.inf, x)
            z = z + jnp.exp(m - top)
        lim = jnp.zeros(s1.shape, F32)
        for a in range(PEER_TOPK):
            bmax = jnp.sum(jnp.where(arow == float(a), sel, 0.0), axis=0, keepdims=True)
            lim = jnp.where(rank1 == float(a), bmax, lim)
        c1_ref[h] = 0.5 * jnp.exp(s1 - v1[0]) / z
        lim_ref[h] = lim
        p2_ref[h] = jnp.exp(s2 - v2[0]).astype(BF16)
        r2_ref[h] = rank2.astype(BF16)

    group = cand_ref.shape[0]

    def head_group(i, carry):
        tied = [head_fast(group * i + j, cand_ref.at[j]) for j in range(group)]
        for j in range(group):
            @pl.when(jnp.max(jnp.where(tied[j], 1.0, 0.0)) > 0.0)
            def _():
                head_exact(group * i + j, cand_ref.at[j])

        return carry

    lax.fori_loop(0, PEER_HEADS // group, head_group, 0)


def _select(sc, tb):
    nsc, nk, t = sc.shape
    flat = np.full((_NCAND_PAD, 1), 1023.0, np.float32)
    arow = np.full((_NCAND_PAD, 1), -1.0, np.float32)
    for i, (a, b) in enumerate(_CAND):
        flat[i, 0] = a * PEER_TOPK + b
        arow[i, 0] = a
    rowsel = jnp.asarray(arow.T == np.arange(PEER_TOPK, dtype=np.float32)[:, None]).astype(BF16)
    flat = jnp.asarray(np.broadcast_to(flat, (_NCAND_PAD, tb)))
    arow = jnp.asarray(np.broadcast_to(arow, (_NCAND_PAD, tb)))
    out = lambda dt: jax.ShapeDtypeStruct((PEER_HEADS, nk, t), dt)
    ospec = pl.BlockSpec((PEER_HEADS, nk, tb), lambda i: (0, 0, i))
    cspec = pl.BlockSpec((_NCAND_PAD, tb), lambda i: (0, 0))
    return pl.pallas_call(
        _select_kernel,
        grid=(t // tb,),
        in_specs=[pl.BlockSpec((nsc, nk, tb), lambda i: (0, 0, i)), cspec, cspec,
                  pl.BlockSpec((PEER_TOPK, _NCAND_PAD), lambda i: (0, 0))],
        out_specs=[ospec, ospec, ospec, ospec],
        out_shape=[out(F32), out(F32), out(BF16), out(BF16)],
        scratch_shapes=[pltpu.VMEM((2, _NCAND_PAD, tb), F32)],
        compiler_params=_cparams("parallel"),
        name="select",
    )(sc, flat, arow, rowsel)


def _experts_kernel(xnt_ref, u_ref, vt_ref, c1_ref, lim_ref, p2_ref, r2_ref, h_ref, gfin_ref, o_ref,
                    acc_ref, g_ref, *, nblk):
    s = pl.program_id(0)
    e = lax.rem(s, nblk)
    rows = c1_ref.shape[1]
    nk = PEER_NKEYS

    slot = lax.rem(s, 2)
    g_new = g_ref.at[slot]
    g_old = g_ref.at[1 - slot]

    @pl.when(s == 0)
    def _():
        acc_ref[...] = jnp.zeros_like(acc_ref)
        g_old[...] = jnp.zeros_like(g_old)

    def row_bf16(ref, h, r):
        x = jnp.broadcast_to(ref[h, r:r + 1, :], (16, ref.shape[2])).astype(BF16)
        return jnp.concatenate([x] * (nk // 16), axis=0)

    mc = min(512, u_ref.shape[0])
    a = jnp.concatenate([_dot(u_ref[i:i + mc, :], xnt_ref[...]) for i in range(0, u_ref.shape[0], mc)], axis=0)
    contrib = _dot(vt_ref[...], g_old[...])
    acc_ref[...] = jnp.where(lax.rem(s + nblk - 1, nblk) == 0, contrib, acc_ref[...] + contrib)
    for r in range(rows):
        w = None
        for h in range(PEER_HEADS):
            term = jnp.where(r2_ref[h] < row_bf16(lim_ref, h, r), p2_ref[h], jnp.zeros((), BF16)) \
                * row_bf16(c1_ref, h, r)
            w = term if w is None else w + term
        ab = a[r * nk:(r + 1) * nk].astype(BF16)
        g_new[r * nk:(r + 1) * nk, :] = ab * (1.0 + lax.erf(ab * INV_SQRT2)) * w

    @pl.when(jnp.logical_and(e == 0, s > 0))
    def _():
        hh = h_ref[...] + acc_ref[...].T
        o_ref[...] = hh * lax.rsqrt(jnp.mean(hh * hh, axis=-1, keepdims=True) + EPS) * gfin_ref[...]


def _experts(xnt, u_bf, vt_blk, c1, lim, p2, r2, h, gfin, tt):
    t, d = h.shape
    nblk, _, et = vt_blk.shape
    rows = et // PEER_NKEYS
    nt = t // tt
    tile = lambda s: jnp.minimum(s // nblk, nt - 1)
    blk = lambda s: lax.rem(s, nblk)
    prev_tile = lambda s: jnp.maximum(s - 1, 0) // nblk
    prev_blk = lambda s: lax.rem(s + nblk - 1, nblk)
    tok3 = pl.BlockSpec((PEER_HEADS, PEER_NKEYS, tt), lambda s: (0, 0, tile(s)))
    row3 = pl.BlockSpec((PEER_HEADS, rows, tt), lambda s: (0, blk(s), tile(s)))
    return pl.pallas_call(
        functools.partial(_experts_kernel, nblk=nblk),
        grid=(nt * nblk + 1,),
        in_specs=[
            pl.BlockSpec((d, tt), lambda s: (0, tile(s))),
            pl.BlockSpec((et, d), lambda s: (blk(s), 0)),
            pl.BlockSpec((None, d, et), lambda s: (prev_blk(s), 0, 0)),
            row3, row3, tok3, tok3,
            pl.BlockSpec((tt, d), lambda s: (prev_tile(s), 0)),
            pl.BlockSpec((1, d), lambda s: (0, 0)),
        ],
        out_specs=pl.BlockSpec((tt, d), lambda s: (prev_tile(s), 0)),
        out_shape=jax.ShapeDtypeStruct((t, d), F32),
        scratch_shapes=[pltpu.VMEM((d, tt), F32), pltpu.VMEM((2, et, tt), BF16)],
        compiler_params=_cparams("arbitrary"),
        name="experts",
    )(xnt, u_bf, vt_blk, c1, lim, p2, r2, h, gfin)


def _rope_tables(seq):
    rows = seq // GRID_W
    row = np.broadcast_to(np.arange(rows)[:, None], (rows, GRID_W)).reshape(-1)
    col = np.broadcast_to(np.arange(GRID_W)[None, :], (rows, GRID_W)).reshape(-1)
    half = HEAD_DIM // 4
    freqs = ROPE_THETA ** (-np.arange(half, dtype=np.float64) / half)

    def cs(pos):
        ang = pos.astype(np.float64)[:, None] * freqs[None, :]
        return np.cos(ang), np.sin(ang)

    cr, sr = cs(row)
    cc, sc = cs(col)
    zero = np.zeros_like(sr)
    cos = np.concatenate([cr, cr, cc, cc], axis=1)
    sa = np.concatenate([-sr, zero, -sc, zero], axis=1)
    sb = np.concatenate([zero, sr, zero, sc], axis=1)
    tile = lambda a: jnp.asarray(np.concatenate([a, a], axis=1).astype(np.float32))
    return tile(cos), tile(sa), tile(sb)


def _bucket_table():
    qi = np.arange(Q_BLOCK)
    kj = np.arange(3 * Q_BLOCK)
    rel = kj[None, :] - Q_BLOCK - qi[:, None]
    nbk = N_BUCKETS // 2
    max_exact = nbk // 2
    assert (max_exact, MAX_DISTANCE, nbk - max_exact) == (8, 128, 8), "the integer form below assumes these constants"
    n = np.abs(rel)
    log_part = np.floor(np.log2(np.maximum(n * n // 64, 1))).astype(np.int64)
    log_part = np.where(2 ** (log_part + 1) * 64 <= n * n, log_part + 1, log_part)
    log_part = np.where(2 ** log_part * 64 > n * n, log_part - 1, log_part)
    large = np.minimum(max_exact + log_part, nbk - 1)
    bucket = np.where(rel > 0, nbk, 0) + np.where(n < max_exact, n, large)
    return jnp.asarray(np.where(n <= WINDOW, bucket, N_BUCKETS).astype(np.int32))


def _tile_sizes(seq, tokens):
    tm = min(512, seq)
    tq = min(512, seq)
    tb = min(128, tokens)
    tt = min(512, tokens)
    return tm, tq, tb, tt


def kernel(x, g_mix, w_in, b_gate, q_norm_g, k_norm_g, rel_bias, sink, w_branch_a, w_branch_b, w_out, g_ffn,
           peer_wq, peer_subkeys, peer_u, peer_v, g_final):
    bsz, seq, d = x.shape
    t = bsz * seq
    depth = w_in.shape[0]
    tm, tq, tb, tt = _tile_sizes(seq, t)
    assert depth == 1, "the experts stage fuses the final norm, so only a single layer is supported"
    assert seq % tm == 0 and seq % Q_BLOCK == 0 and t % tb == 0 and t % tt == 0

    cos, sa, sb = _rope_tables(seq)
    bucket = _bucket_table()
    bd = jnp.asarray(np.kron(np.eye(2, dtype=np.float32),
                             np.full((HEAD_DIM, HEAD_DIM), 1.0 / HEAD_DIM, np.float32))).astype(BF16)
    expert_block = 2048

    h = x.reshape(t, d)
    for l in range(depth):
        qg = jnp.tile(q_norm_g[l], 2)[None, :]
        kg = jnp.tile(k_norm_g[l], 2)[None, :]
        qa, ka, va, qbt, kb, vbt, gate = _in_proj(
            h, g_mix[l][None, :], w_in[l].astype(BF16), b_gate[l][None, :], qg, kg, cos, sa, sb, bd, seq, tm)
        oa = _attn_a(qa, ka, va, bsz, seq, tq)
        kb_pad = jnp.pad(kb.reshape(B_KV, bsz, seq, HEAD_DIM), ((0, 0), (0, 0), (Q_BLOCK, Q_BLOCK), (0, 0)))
        vbt_pad = jnp.pad(vbt.reshape(B_KV, 2 * HEAD_DIM, bsz, seq).transpose(0, 2, 1, 3),
                          ((0, 0), (0, 0), (0, 0), (Q_BLOCK, Q_BLOCK)))
        ob = _attn_b(qbt, kb_pad, vbt_pad, bucket.T, rel_bias, sink[l][None, :], bsz, seq)
        skh, skl = _split_bf16(peer_subkeys[l].reshape(2 * PEER_HEADS, PEER_NKEYS, PEER_DQ // 2))
        sk3 = jnp.concatenate([skh, skh, skl], axis=-1)
        h1, xnt, sc = _merge(oa, ob, gate, h, w_branch_a[l].astype(BF16), w_branch_b[l].astype(BF16),
                            w_out[l].astype(BF16), g_ffn[l][None, :], peer_wq[l].astype(BF16), sk3, tm)
        c1, lim, p2, r2 = _select(sc, tb)
        vt_blk = peer_v[l].astype(BF16).reshape(-1, expert_block, d).transpose(0, 2, 1)
        out = _experts(xnt, peer_u[l].astype(BF16), vt_blk, c1, lim, p2, r2, h1, g_final[None, :], tt)
    return out.reshape(bsz, seq, d)
```

```python
import functools
import math

import numpy as np
import jax
import jax.numpy as jnp
from jax import lax
from jax.experimental import pallas as pl
from jax.experimental.pallas import tpu as pltpu

F32 = jnp.float32
BF16 = jnp.bfloat16

HEAD_DIM = 64
A_HEADS, A_KV = 8, 2
B_HEADS, B_KV = 8, 2
Q_BLOCK = 128
WINDOW = 128
GRID_W = 64
ROPE_THETA = 10000.0
N_BUCKETS = 32
MAX_DISTANCE = 128
PEER_HEADS = 8
PEER_NKEYS = 128
PEER_TOPK = 16
PEER_DQ = 256
EPS = 1e-6
NEG = -1e30
INV_SQRT2 = 0.7071067811865476
LOG2E = 1.4426950408889634

V7X_LANES = 128
V7X_VMEM_LIMIT_BYTES = 56 * 1024 * 1024

_CAND = [(a, b) for a in range(PEER_TOPK) for b in range(PEER_TOPK) if (a + 1) * (b + 1) <= PEER_TOPK]
_NCAND = len(_CAND)
_NCAND_PAD = -(-_NCAND // 8) * 8


def _cparams(*sem):
    return pltpu.CompilerParams(dimension_semantics=sem, vmem_limit_bytes=V7X_VMEM_LIMIT_BYTES)


def _split_bf16(x):
    hi = x.astype(BF16)
    lo = (x - hi.astype(F32)).astype(BF16)
    return hi, lo


def _dot(a, b):
    return jnp.dot(a, b, preferred_element_type=F32)


def _dot_nt(a, b):
    return lax.dot_general(a, b, (((1,), (1,)), ((), ())), preferred_element_type=F32)


def _in_proj_kernel(x_ref, gmix_ref, w_ref, bg_ref, qg_ref, kg_ref, cos_ref, sa_ref, sb_ref, bd_ref,
                    qa_ref, ka_ref, va_ref, qbt_ref, kb_ref, vbt_ref, gate_ref):
    d = x_ref.shape[1]
    x = x_ref[...]
    n = x * lax.rsqrt(jnp.mean(x * x, axis=-1, keepdims=True) + EPS) * gmix_ref[...]
    nb = n.astype(BF16)
    cos, sa, sb, bd = cos_ref[...], sa_ref[...], sb_ref[...], bd_ref[...]

    def norm_rope(q, gain):
        hi, lo = _split_bf16(q * q)
        ms = _dot(hi, bd) + _dot(lo, bd)
        y = q * lax.rsqrt(ms + EPS) * gain
        return y * cos + pltpu.roll(y, V7X_LANES - 16, 1) * sa + pltpu.roll(y, 16, 1) * sb

    def put_heads(ref, first, y):
        ref[first] = y[:, :HEAD_DIM].astype(BF16)
        ref[first + 1] = y[:, HEAD_DIM:].astype(BF16)

    def put_values(ref, y):
        lane = lax.broadcasted_iota(jnp.int32, y.shape, 1)
        tail = jnp.where(lane == HEAD_DIM, 1.0, 0.0)
        ref[0] = jnp.where(lane < HEAD_DIM, y, tail).astype(BF16)
        ref[1] = jnp.where(lane < HEAD_DIM, pltpu.roll(y, HEAD_DIM, 1), tail).astype(BF16)

    qscale = HEAD_DIM ** -0.5 * LOG2E
    n_attn = (A_HEADS + 2 * A_KV + B_HEADS + 2 * B_KV) * HEAD_DIM
    z = _dot(nb, w_ref[:, :n_attn])
    col = lambda c: z[:, c * 128:(c + 1) * 128]
    c = 0
    for j in range(A_HEADS // 2):
        put_heads(qa_ref, 2 * j, norm_rope(col(c), qg_ref[...]) * qscale)
        c += 1
    put_heads(ka_ref, 0, norm_rope(col(c), kg_ref[...]))
    put_values(va_ref, col(c + 1))
    c += 2
    nqb = B_HEADS // 2
    qbt_ref[...] = (z[:, c * 128:(c + nqb) * 128] * qscale).T.astype(BF16)
    c += nqb
    put_heads(kb_ref, 0, col(c))
    y = col(c + 1)
    lane = lax.broadcasted_iota(jnp.int32, y.shape, 1)
    tail = jnp.where(lane == HEAD_DIM, 1.0, 0.0)
    vbt_ref[0] = jnp.where(lane < HEAD_DIM, y, tail).T.astype(BF16)
    vbt_ref[1] = jnp.where(lane < HEAD_DIM, pltpu.roll(y, HEAD_DIM, 1), tail).T.astype(BF16)
    for j in range(2 * d // 512):
        lo, hi = n_attn + j * 512, n_attn + (j + 1) * 512
        zg = _dot(nb, w_ref[:, lo:hi]) + bg_ref[:, j * 512:(j + 1) * 512]
        gate_ref[:, j * 512:(j + 1) * 512] = (1.0 / (1.0 + jnp.exp(-zg))).astype(BF16)


def _in_proj(x2, gmix, w_bf, bg, qg, kg, cos, sa, sb, bd, seq, tm):
    t, d = x2.shape
    nt = t // tm
    ns = seq // tm
    const = lambda i: (0, 0)
    tab = pl.BlockSpec((tm, 128), lambda i: (i % ns, 0))
    head = lambda nh: pl.BlockSpec((nh, tm, HEAD_DIM), lambda i: (0, i, 0))
    wide = lambda nh: pl.BlockSpec((nh, tm, 2 * HEAD_DIM), lambda i: (0, i, 0))
    return pl.pallas_call(
        _in_proj_kernel,
        grid=(nt,),
        in_specs=[
            pl.BlockSpec((tm, d), lambda i: (i, 0)),
            pl.BlockSpec((1, d), const),
            pl.BlockSpec(w_bf.shape, const),
            pl.BlockSpec((1, 2 * d), const),
            pl.BlockSpec((1, 128), const),
            pl.BlockSpec((1, 128), const),
            tab, tab, tab,
            pl.BlockSpec((128, 128), const),
        ],
        out_specs=[head(A_HEADS), head(A_KV), wide(A_KV),
                   pl.BlockSpec((B_HEADS * HEAD_DIM, tm), lambda i: (0, i)), head(B_KV),
                   pl.BlockSpec((B_KV, 2 * HEAD_DIM, tm), lambda i: (0, 0, i)),
                   pl.BlockSpec((tm, 2 * d), lambda i: (i, 0))],
        out_shape=[
            jax.ShapeDtypeStruct((A_HEADS, t, HEAD_DIM), BF16),
            jax.ShapeDtypeStruct((A_KV, t, HEAD_DIM), BF16),
            jax.ShapeDtypeStruct((A_KV, t, 2 * HEAD_DIM), BF16),
            jax.ShapeDtypeStruct((B_HEADS * HEAD_DIM, t), BF16),
            jax.ShapeDtypeStruct((B_KV, t, HEAD_DIM), BF16),
            jax.ShapeDtypeStruct((B_KV, 2 * HEAD_DIM, t), BF16),
            jax.ShapeDtypeStruct((t, 2 * d), BF16),
        ],
        compiler_params=_cparams("parallel"),
        name="in_proj",
    )(x2, gmix, w_bf, bg, qg, kg, cos, sa, sb, bd)


def _attn_a_kernel(q_ref, k_ref, v_ref, o_ref, *, kc):
    r, tq, hd = q_ref.shape
    seq = k_ref.shape[0]
    q = q_ref[...].reshape(r * tq, hd)
    m = jnp.full((r * tq, 1), -jnp.inf, F32)
    acc = jnp.zeros((r * tq, v_ref.shape[1]), F32)
    for c in range(seq // kc):
        s = _dot_nt(q, k_ref[c * kc:(c + 1) * kc, :])
        m_new = jnp.maximum(m, jnp.max(s, axis=-1, keepdims=True))
        p = jnp.exp2(s - m_new)
        acc = jnp.exp2(m - m_new) * acc + _dot(p.astype(BF16), v_ref[c * kc:(c + 1) * kc, :])
        m = m_new
    o = acc[:, :hd] / acc[:, hd:hd + 1]
    for j in range(r):
        o_ref[:, j * hd:(j + 1) * hd] = o[j * tq:(j + 1) * tq].astype(BF16)


def _attn_a(qa, ka, va, bsz, seq, tq):
    t = qa.shape[1]
    r = A_HEADS // A_KV
    nq = seq // tq
    return pl.pallas_call(
        functools.partial(_attn_a_kernel, kc=min(512, seq)),
        grid=(bsz, A_KV, nq),
        in_specs=[
            pl.BlockSpec((r, tq, HEAD_DIM), lambda b, g, i: (g, b * nq + i, 0)),
            pl.BlockSpec((None, seq, HEAD_DIM), lambda b, g, i: (g, b, 0)),
            pl.BlockSpec((None, seq, 2 * HEAD_DIM), lambda b, g, i: (g, b, 0)),
        ],
        out_specs=pl.BlockSpec((tq, r * HEAD_DIM), lambda b, g, i: (b * nq + i, g)),
        out_shape=jax.ShapeDtypeStruct((t, A_HEADS * HEAD_DIM), BF16),
        compiler_params=_cparams("parallel", "parallel", "parallel"),
        name="attn_a",
    )(qa, ka, va)


def _attn_b_kernel(relb_ref, sink_ref, qt_ref, k_ref, vt_ref, bucket_t_ref, o_ref, bias_ref):
    b = pl.program_id(0)
    n = pl.program_id(1)
    nblk = pl.num_programs(1)
    r = B_HEADS // B_KV
    qb = Q_BLOCK
    kw = 3 * Q_BLOCK
    hd = HEAD_DIM

    @pl.when(jnp.logical_and(b == 0, n == 0))
    def _():
        for v in range(bucket_t_ref.shape[0]):
            bucket_t = bucket_t_ref[v]
            for h in range(B_HEADS):
                acc = jnp.full((kw, qb), NEG, F32)
                for k in range(N_BUCKETS):
                    acc = jnp.where(bucket_t == k, relb_ref[k, h] * LOG2E, acc)
                bias_ref[v, h // r, :, (h % r) * qb:(h % r + 1) * qb] = acc

    nq = qt_ref.shape[1] // qb
    last = nblk * nq - 1
    head_of_lane = lax.broadcasted_iota(jnp.int32, (1, r * qb), 1) // qb
    for i in range(nq):
        blk = n * nq + i
        variant = jnp.where(blk == 0, 0, jnp.where(blk == last, 2, 1))
        start = pl.multiple_of(jnp.clip(blk - 1, 0, last - 2) * qb, qb)
        for g in range(B_KV):
            kwin = k_ref[g, pl.ds(start, kw), :]
            vtw = vt_ref[g, :, pl.ds(start, kw)]
            qt = jnp.concatenate(
                [qt_ref[(g * r + j) * hd:(g * r + j + 1) * hd, i * qb:(i + 1) * qb] for j in range(r)], axis=1)
            st = _dot(kwin, qt) + bias_ref[variant, g]
            sink = jnp.zeros((1, r * qb), F32)
            for j in range(r):
                sink = jnp.where(head_of_lane == j, sink_ref[0, g * r + j] * LOG2E, sink)
            m = jnp.maximum(jnp.max(st, axis=0, keepdims=True), sink)
            ot = _dot(vtw, jnp.exp2(st - m).astype(BF16))
            ot = ot / (ot[hd:hd + 1] + jnp.exp2(sink - m))
            for j in range(r):
                h = g * r + j
                o_ref[i * qb:(i + 1) * qb, h * hd:(h + 1) * hd] = ot[:, j * qb:(j + 1) * qb].T[:, :hd].astype(BF16)


def _attn_b(qbt, kb, vbt, bucket_t, rel_bias, sink, bsz, seq):
    t = qbt.shape[1]
    nb = seq // Q_BLOCK
    assert nb >= 3, "the shifted edge windows need at least three query blocks per sequence"
    nq = 2 if nb % 2 == 0 else 1
    ns = nb // nq
    r = B_HEADS // B_KV
    return pl.pallas_call(
        _attn_b_kernel,
        grid=(bsz, ns),
        in_specs=[
            pl.BlockSpec(memory_space=pltpu.SMEM),
            pl.BlockSpec(memory_space=pltpu.SMEM),
            pl.BlockSpec((B_HEADS * HEAD_DIM, nq * Q_BLOCK), lambda b, n: (0, b * ns + n)),
            pl.BlockSpec((B_KV, seq, HEAD_DIM), lambda b, n: (0, b, 0)),
            pl.BlockSpec((B_KV, 2 * HEAD_DIM, seq), lambda b, n: (0, 0, b)),
            pl.BlockSpec(bucket_t.shape, lambda b, n: (0, 0, 0)),
        ],
        out_specs=pl.BlockSpec((nq * Q_BLOCK, B_HEADS * HEAD_DIM), lambda b, n: (b * ns + n, 0)),
        out_shape=jax.ShapeDtypeStruct((t, B_HEADS * HEAD_DIM), BF16),
        scratch_shapes=[pltpu.VMEM((bucket_t.shape[0], B_KV, 3 * Q_BLOCK, r * Q_BLOCK), F32)],
        compiler_params=_cparams("arbitrary", "arbitrary"),
        name="attn_b",
    )(rel_bias, sink, qbt, kb, vbt, bucket_t)


def _merge_kernel(oa_ref, ob_ref, gate_ref, x_ref, wa_ref, wb_ref, wo_ref, gffn_ref, wq_ref, sk3_ref,
                  h_ref, xnt_ref, sc_ref):
    d = x_ref.shape[1]
    pa = _dot(oa_ref[...], wa_ref[...])
    pb = _dot(ob_ref[...], wb_ref[...])
    mix = gate_ref[:, :d].astype(F32) * pa + gate_ref[:, d:].astype(F32) * pb
    h = x_ref[...] + _dot(mix.astype(BF16), wo_ref[...])
    h_ref[...] = h
    xn = h * lax.rsqrt(jnp.mean(h * h, axis=-1, keepdims=True) + EPS) * gffn_ref[...]
    xnb = xn.astype(BF16)
    xnt_ref[...] = xn.T.astype(BF16)
    half = PEER_DQ // 2
    qh, ql = _split_bf16(_dot(xnb, wq_ref[...]))
    for hc in range(2 * PEER_HEADS):
        cols = slice(hc * half, (hc + 1) * half)
        q3 = jnp.concatenate([qh[:, cols], ql[:, cols], qh[:, cols]], axis=1)
        sc_ref[hc] = _dot_nt(sk3_ref[hc], q3)


def _merge(oa, ob, gate, x2, wa, wb, wo, gffn, wq, sk3, tm):
    t, d = x2.shape
    const2 = lambda i: (0, 0)
    const3 = lambda i: (0, 0, 0)
    row = lambda w: pl.BlockSpec((tm, w), lambda i: (i, 0))
    nsc = 2 * PEER_HEADS
    return pl.pallas_call(
        _merge_kernel,
        grid=(t // tm,),
        in_specs=[
            row(oa.shape[1]), row(ob.shape[1]), row(2 * d), row(d),
            pl.BlockSpec(wa.shape, const2), pl.BlockSpec(wb.shape, const2), pl.BlockSpec(wo.shape, const2),
            pl.BlockSpec((1, d), const2), pl.BlockSpec(wq.shape, const2),
            pl.BlockSpec(sk3.shape, const3),
        ],
        out_specs=[row(d), pl.BlockSpec((d, tm), lambda i: (0, i)),
                   pl.BlockSpec((nsc, PEER_NKEYS, tm), lambda i: (0, 0, i))],
        out_shape=[
            jax.ShapeDtypeStruct((t, d), F32),
            jax.ShapeDtypeStruct((d, t), BF16),
            jax.ShapeDtypeStruct((nsc, PEER_NKEYS, t), F32),
        ],
        compiler_params=_cparams("parallel"),
        name="merge",
    )(oa, ob, gate, x2, wa, wb, wo, gffn, wq, sk3)


def _pop_max(x, order, exact):
    m = jnp.max(x, axis=0, keepdims=True)
    if exact:
        first = jnp.min(jnp.where(x == m, order, jnp.inf), axis=0, keepdims=True)
        return m, order == first
    return m, x == m


def _topk_ranks(s):
    rows = lax.broadcasted_iota(jnp.int32, s.shape, 0).astype(F32)
    rank = jnp.full(s.shape, float(PEER_TOPK), F32)
    x = s
    vals = []
    for a in range(PEER_TOPK):
        m, hit = _pop_max(x, rows, True)
        rank = jnp.where(hit, float(a), rank)
        x = jnp.where(hit, -jnp.inf, x)
        vals.append(m)
    return vals, rank


def _batcher_layers(n):
    layers = []
    p = 1
    while p < n:
        k = p
        while k >= 1:
            layer = []
            for j in range(k % p, n - k, 2 * k):
                for i in range(min(k, n - j - k)):
                    if (i + j) // (2 * p) == (i + j + k) // (2 * p):
                        layer.append((i + j, i + j + k))
            layers.append(layer)
            k //= 2
        p *= 2
    return layers


_SORT16 = _batcher_layers(PEER_TOPK)


def _exchange(xs, i, j):
    xs[i], xs[j] = jnp.maximum(xs[i], xs[j]), jnp.minimum(xs[i], xs[j])


def _sorted_top16(xs):
    xs = list(xs)
    n = len(xs)
    for layer in _SORT16:
        for i, j in layer:
            _exchange(xs, i, j)
    for shift in (4, 2, 1):
        other = [pltpu.roll(x, shift, 0) for x in xs]
        xs = [jnp.maximum(xs[p], other[n - 1 - p]) for p in range(n)]
        stride = n // 2
        while stride >= 1:
            for base in range(0, n, 2 * stride):
                for i in range(base, base + stride):
                    _exchange(xs, i, i + stride)
            stride //= 2
    return xs


def _select_kernel(sc_ref, flat_ref, arow_ref, rowsel_ref, c1_ref, lim_ref, p2_ref, r2_ref, cand_ref):
    tb = sc_ref.shape[2]
    flat = flat_ref[...]
    arow = arow_ref[...]
    cand_ref[...] = jnp.full(cand_ref.shape, -jnp.inf, F32)
    nv = PEER_NKEYS // 8
    k16 = float(PEER_TOPK)

    def head_fast(h, cand):
        s1 = [sc_ref[2 * h, 8 * k:8 * k + 8, :] for k in range(nv)]
        s2 = [sc_ref[2 * h + 1, 8 * k:8 * k + 8, :] for k in range(nv)]
        v1 = _sorted_top16(s1)
        v2 = _sorted_top16(s2)
        r1 = [v[0:1] for v in v1]
        r2 = [v[0:1] for v in v2]
        for i, (a, b) in enumerate(_CAND):
            cand[i:i + 1, :] = r1[a] + r2[b]
        top = r1[0] + r2[0]
        sums = cand[...]
        x = sums
        for _ in range(PEER_TOPK):
            m, hit = _pop_max(x, flat, False)
            x = jnp.where(hit, -jnp.inf, x)
        sel = sums >= m
        z = jnp.sum(jnp.where(sel, jnp.exp(sums - top), 0.0), axis=0, keepdims=True)
        counts = _dot(rowsel_ref[...], jnp.where(sel, 1.0, 0.0).astype(BF16))
        bmax = [jnp.broadcast_to(counts[a:a + 1], (8, tb)) for a in range(PEER_TOPK)]
        lim, rank2 = [], []
        for k in range(nv):
            l = jnp.zeros((8, tb), F32)
            r = jnp.full((8, tb), k16, F32)
            for a in range(PEER_TOPK - 1, -1, -1):
                l = jnp.where(s1[k] >= v1[a], bmax[a], l)
                r = jnp.where(s2[k] >= v2[a], float(a), r)
            lim.append(l)
            rank2.append(r)
        c1_ref[h] = 0.5 * jnp.exp(jnp.concatenate(s1, axis=0) - r1[0]) / z
        lim_ref[h] = jnp.concatenate(lim, axis=0)
        p2_ref[h] = jnp.exp(jnp.concatenate(s2, axis=0) - r2[0]).astype(BF16)
        r2_ref[h] = jnp.concatenate(rank2, axis=0).astype(BF16)
        tied = jnp.sum(counts, axis=0, keepdims=True) != k16
        for s, v, r in ((s1, v1, r1), (s2, v2, r2)):
            count = sum(jnp.where(x >= v[-1], 1.0, 0.0) for x in s)
            tied = tied | (jnp.sum(count, axis=0, keepdims=True) != k16)
            for a in range(PEER_TOPK - 1):
                tied = tied | (r[a] == r[a + 1])
        return tied

    def head_exact(h, cand):
        s1 = sc_ref[2 * h]
        s2 = sc_ref[2 * h + 1]
        v1, rank1 = _topk_ranks(s1)
        v2, rank2 = _topk_ranks(s2)
        for i, (a, b) in enumerate(_CAND):
            cand[i:i + 1, :] = v1[a] + v2[b]
        top = v1[0] + v2[0]
        sel = jnp.zeros(cand.shape, F32)
        z = jnp.zeros((1, tb), F32)
        x = cand[...]
        for _ in range(PEER_TOPK):
            m, hit = _pop_max(x, flat, True)
            sel = jnp.where(hit, 1.0, sel)
            x = jnp.where(hit, -jnp.inf, x)
            z = z + jnp.exp(m - top)
        lim = jnp.zeros(s1.shape, F32)
        for a in range(PEER_TOPK):
            bmax = jnp.sum(jnp.where(arow == float(a), sel, 0.0), axis=0, keepdims=True)
            lim = jnp.where(rank1 == float(a), bmax, lim)
        c1_ref[h] = 0.5 * jnp.exp(s1 - v1[0]) / z
        lim_ref[h] = lim
        p2_ref[h] = jnp.exp(s2 - v2[0]).astype(BF16)
        r2_ref[h] = rank2.astype(BF16)

    group = cand_ref.shape[0]

    def head_group(i, carry):
        tied = [head_fast(group * i + j, cand_ref.at[j]) for j in range(group)]
        for j in range(group):
            @pl.when(jnp.max(jnp.where(tied[j], 1.0, 0.0)) > 0.0)
            def _():
                head_exact(group * i + j, cand_ref.at[j])

        return carry

    lax.fori_loop(0, PEER_HEADS // group, head_group, 0)


def _select(sc, tb):
    nsc, nk, t = sc.shape
    flat = np.full((_NCAND_PAD, 1), 1023.0, np.float32)
    arow = np.full((_NCAND_PAD, 1), -1.0, np.float32)
    for i, (a, b) in enumerate(_CAND):
        flat[i, 0] = a * PEER_TOPK + b
        arow[i, 0] = a
    rowsel = jnp.asarray(arow.T == np.arange(PEER_TOPK, dtype=np.float32)[:, None]).astype(BF16)
    flat = jnp.asarray(np.broadcast_to(flat, (_NCAND_PAD, tb)))
    arow = jnp.asarray(np.broadcast_to(arow, (_NCAND_PAD, tb)))
    out = lambda dt: jax.ShapeDtypeStruct((PEER_HEADS, nk, t), dt)
    ospec = pl.BlockSpec((PEER_HEADS, nk, tb), lambda i: (0, 0, i))
    cspec = pl.BlockSpec((_NCAND_PAD, tb), lambda i: (0, 0))
    return pl.pallas_call(
        _select_kernel,
        grid=(t // tb,),
        in_specs=[pl.BlockSpec((nsc, nk, tb), lambda i: (0, 0, i)), cspec, cspec,
                  pl.BlockSpec((PEER_TOPK, _NCAND_PAD), lambda i: (0, 0))],
        out_specs=[ospec, ospec, ospec, ospec],
        out_shape=[out(F32), out(F32), out(BF16), out(BF16)],
        scratch_shapes=[pltpu.VMEM((2, _NCAND_PAD, tb), F32)],
        compiler_params=_cparams("parallel"),
        name="select",
    )(sc, flat, arow, rowsel)


def _experts_kernel(xnt_ref, u_ref, vt_ref, c1_ref, lim_ref, p2_ref, r2_ref, h_ref, gfin_ref, o_ref,
                    acc_ref, g_ref, *, nblk):
    s = pl.program_id(0)
    e = lax.rem(s, nblk)
    rows = c1_ref.shape[1]
    nk = PEER_NKEYS

    slot = lax.rem(s, 2)
    g_new = g_ref.at[slot]
    g_old = g_ref.at[1 - slot]

    @pl.when(s == 0)
    def _():
        acc_ref[...] = jnp.zeros_like(acc_ref)
        g_old[...] = jnp.zeros_like(g_old)

    def row_bf16(ref, h, r):
        x = jnp.broadcast_to(ref[h, r:r + 1, :], (16, ref.shape[2])).astype(BF16)
        return jnp.concatenate([x] * (nk // 16), axis=0)

    mc = min(512, u_ref.shape[0])
    a = jnp.concatenate([_dot(u_ref[i:i + mc, :], xnt_ref[...]) for i in range(0, u_ref.shape[0], mc)], axis=0)
    contrib = _dot(vt_ref[...], g_old[...])
    acc_ref[...] = jnp.where(lax.rem(s + nblk - 1, nblk) == 0, contrib, acc_ref[...] + contrib)
    for r in range(rows):
        w = None
        for h in range(PEER_HEADS):
            term = jnp.where(r2_ref[h] < row_bf16(lim_ref, h, r), p2_ref[h], jnp.zeros((), BF16)) \
                * row_bf16(c1_ref, h, r)
            w = term if w is None else w + term
        ab = a[r * nk:(r + 1) * nk].astype(BF16)
        g_new[r * nk:(r + 1) * nk, :] = ab * (1.0 + lax.erf(ab * INV_SQRT2)) * w

    @pl.when(jnp.logical_and(e == 0, s > 0))
    def _():
        hh = h_ref[...] + acc_ref[...].T
        o_ref[...] = hh * lax.rsqrt(jnp.mean(hh * hh, axis=-1, keepdims=True) + EPS) * gfin_ref[...]


def _experts(xnt, u_bf, vt_blk, c1, lim, p2, r2, h, gfin, tt):
    t, d = h.shape
    nblk, _, et = vt_blk.shape
    rows = et // PEER_NKEYS
    nt = t // tt
    tile = lambda s: jnp.minimum(s // nblk, nt - 1)
    blk = lambda s: lax.rem(s, nblk)
    prev_tile = lambda s: jnp.maximum(s - 1, 0) // nblk
    prev_blk = lambda s: lax.rem(s + nblk - 1, nblk)
    tok3 = pl.BlockSpec((PEER_HEADS, PEER_NKEYS, tt), lambda s: (0, 0, tile(s)))
    row3 = pl.BlockSpec((PEER_HEADS, rows, tt), lambda s: (0, blk(s), tile(s)))
    return pl.pallas_call(
        functools.partial(_experts_kernel, nblk=nblk),
        grid=(nt * nblk + 1,),
        in_specs=[
            pl.BlockSpec((d, tt), lambda s: (0, tile(s))),
            pl.BlockSpec((et, d), lambda s: (blk(s), 0)),
            pl.BlockSpec((None, d, et), lambda s: (prev_blk(s), 0, 0)),
            row3, row3, tok3, tok3,
            pl.BlockSpec((tt, d), lambda s: (prev_tile(s), 0)),
            pl.BlockSpec((1, d), lambda s: (0, 0)),
        ],
        out_specs=pl.BlockSpec((tt, d), lambda s: (prev_tile(s), 0)),
        out_shape=jax.ShapeDtypeStruct((t, d), F32),
        scratch_shapes=[pltpu.VMEM((d, tt), F32), pltpu.VMEM((2, et, tt), BF16)],
        compiler_params=_cparams("arbitrary"),
        name="experts",
    )(xnt, u_bf, vt_blk, c1, lim, p2, r2, h, gfin)


def _rope_tables(seq):
    rows = seq // GRID_W
    row = np.broadcast_to(np.arange(rows)[:, None], (rows, GRID_W)).reshape(-1)
    col = np.broadcast_to(np.arange(GRID_W)[None, :], (rows, GRID_W)).reshape(-1)
    half = HEAD_DIM // 4
    freqs = ROPE_THETA ** (-np.arange(half, dtype=np.float64) / half)

    def cs(pos):
        ang = pos.astype(np.float64)[:, None] * freqs[None, :]
        return np.cos(ang), np.sin(ang)

    cr, sr = cs(row)
    cc, sc = cs(col)
    zero = np.zeros_like(sr)
    cos = np.concatenate([cr, cr, cc, cc], axis=1)
    sa = np.concatenate([-sr, zero, -sc, zero], axis=1)
    sb = np.concatenate([zero, sr, zero, sc], axis=1)
    tile = lambda a: jnp.asarray(np.concatenate([a, a], axis=1).astype(np.float32))
    return tile(cos), tile(sa), tile(sb)


def _bucket_tables():
    qi = np.arange(Q_BLOCK)
    kj = np.arange(3 * Q_BLOCK)
    rel = kj[None, :, None] - Q_BLOCK * np.arange(3)[:, None, None] - qi[None, None, :]
    nbk = N_BUCKETS // 2
    max_exact = nbk // 2
    assert (max_exact, MAX_DISTANCE, nbk - max_exact) == (8, 128, 8), "the integer form below assumes these constants"
    n = np.abs(rel)
    log_part = np.floor(np.log2(np.maximum(n * n // 64, 1))).astype(np.int64)
    log_part = np.where(2 ** (log_part + 1) * 64 <= n * n, log_part + 1, log_part)
    log_part = np.where(2 ** log_part * 64 > n * n, log_part - 1, log_part)
    large = np.minimum(max_exact + log_part, nbk - 1)
    bucket = np.where(rel > 0, nbk, 0) + np.where(n < max_exact, n, large)
    return jnp.asarray(np.where(n <= WINDOW, bucket, N_BUCKETS).astype(np.int32))


def _tile_sizes(seq, tokens):
    tm = min(512, seq)
    tq = min(512, seq)
    tb = min(128, tokens)
    tt = min(512, tokens)
    return tm, tq, tb, tt


def kernel(x, g_mix, w_in, b_gate, q_norm_g, k_norm_g, rel_bias, sink, w_branch_a, w_branch_b, w_out, g_ffn,
           peer_wq, peer_subkeys, peer_u, peer_v, g_final):
    bsz, seq, d = x.shape
    t = bsz * seq
    depth = w_in.shape[0]
    tm, tq, tb, tt = _tile_sizes(seq, t)
    assert depth == 1, "the experts stage fuses the final norm, so only a single layer is supported"
    assert seq % tm == 0 and seq % Q_BLOCK == 0 and t % tb == 0 and t % tt == 0

    cos, sa, sb = _rope_tables(seq)
    bucket_t = _bucket_tables()
    bd = jnp.asarray(np.kron(np.eye(2, dtype=np.float32),
                             np.full((HEAD_DIM, HEAD_DIM), 1.0 / HEAD_DIM, np.float32))).astype(BF16)
    expert_block = 2048

    h = x.reshape(t, d)
    for l in range(depth):
        qg = jnp.tile(q_norm_g[l], 2)[None, :]
        kg = jnp.tile(k_norm_g[l], 2)[None, :]
        qa, ka, va, qbt, kb, vbt, gate = _in_proj(
            h, g_mix[l][None, :], w_in[l].astype(BF16), b_gate[l][None, :], qg, kg, cos, sa, sb, bd, seq, tm)
        oa = _attn_a(qa, ka, va, bsz, seq, tq)
        ob = _attn_b(qbt, kb, vbt, bucket_t, rel_bias, sink[l][None, :], bsz, seq)
        skh, skl = _split_bf16(peer_subkeys[l].reshape(2 * PEER_HEADS, PEER_NKEYS, PEER_DQ // 2))
        sk3 = jnp.concatenate([skh, skh, skl], axis=-1)
        h1, xnt, sc = _merge(oa, ob, gate, h, w_branch_a[l].astype(BF16), w_branch_b[l].astype(BF16),
                            w_out[l].astype(BF16), g_ffn[l][None, :], peer_wq[l].astype(BF16), sk3, tm)
        c1, lim, p2, r2 = _select(sc, tb)
        vt_blk = peer_v[l].astype(BF16).reshape(-1, expert_block, d).transpose(0, 2, 1)
        out = _experts(xnt, peer_u[l].astype(BF16), vt_blk, c1, lim, p2, r2, h1, g_final[None, :], tt)
    return out.reshape(bsz, seq, d)
```

```python
import functools
import math

import numpy as np
import jax
import jax.numpy as jnp
from jax import lax
from jax.experimental import pallas as pl
from jax.experimental.pallas import tpu as pltpu

F32 = jnp.float32
BF16 = jnp.bfloat16

HEAD_DIM = 64
A_HEADS, A_KV = 8, 2
B_HEADS, B_KV = 8, 2
Q_BLOCK = 128
WINDOW = 128
GRID_W = 64
ROPE_THETA = 10000.0
N_BUCKETS = 32
MAX_DISTANCE = 128
PEER_HEADS = 8
PEER_NKEYS = 128
PEER_TOPK = 16
PEER_DQ = 256
EPS = 1e-6
NEG = -1e30
INV_SQRT2 = 0.7071067811865476
LOG2E = 1.4426950408889634

V7X_LANES = 128
V7X_VMEM_LIMIT_BYTES = 56 * 1024 * 1024

_CAND = [(a, b) for a in range(PEER_TOPK) for b in range(PEER_TOPK) if (a + 1) * (b + 1) <= PEER_TOPK]
_NCAND = len(_CAND)
_NCAND_PAD = -(-_NCAND // 8) * 8


def _cparams(*sem):
    return pltpu.CompilerParams(dimension_semantics=sem, vmem_limit_bytes=V7X_VMEM_LIMIT_BYTES)


def _split_bf16(x):
    hi = x.astype(BF16)
    lo = (x - hi.astype(F32)).astype(BF16)
    return hi, lo


def _dot(a, b):
    return jnp.dot(a, b, preferred_element_type=F32)


def _dot_nt(a, b):
    return lax.dot_general(a, b, (((1,), (1,)), ((), ())), preferred_element_type=F32)


def _in_proj_kernel(x_ref, gmix_ref, w_ref, bg_ref, qg_ref, kg_ref, cos_ref, sa_ref, sb_ref, bd_ref,
                    qa_ref, ka_ref, va_ref, qbt_ref, kb_ref, vbt_ref, gate_ref):
    d = x_ref.shape[1]
    x = x_ref[...]
    n = x * lax.rsqrt(jnp.mean(x * x, axis=-1, keepdims=True) + EPS) * gmix_ref[...]
    nb = n.astype(BF16)
    cos, sa, sb, bd = cos_ref[...], sa_ref[...], sb_ref[...], bd_ref[...]

    def norm_rope(q, gain):
        hi, lo = _split_bf16(q * q)
        ms = _dot(hi, bd) + _dot(lo, bd)
        y = q * lax.rsqrt(ms + EPS) * gain
        return y * cos + pltpu.roll(y, V7X_LANES - 16, 1) * sa + pltpu.roll(y, 16, 1) * sb

    def put_heads(ref, first, y):
        ref[first] = y[:, :HEAD_DIM].astype(BF16)
        ref[first + 1] = y[:, HEAD_DIM:].astype(BF16)

    def put_values(ref, y):
        lane = lax.broadcasted_iota(jnp.int32, y.shape, 1)
        tail = jnp.where(lane == HEAD_DIM, 1.0, 0.0)
        ref[0] = jnp.where(lane < HEAD_DIM, y, tail).astype(BF16)
        ref[1] = jnp.where(lane < HEAD_DIM, pltpu.roll(y, HEAD_DIM, 1), tail).astype(BF16)

    qscale = HEAD_DIM ** -0.5 * LOG2E
    n_attn = (A_HEADS + 2 * A_KV + B_HEADS + 2 * B_KV) * HEAD_DIM
    z = _dot(nb, w_ref[:, :n_attn])
    col = lambda c: z[:, c * 128:(c + 1) * 128]
    c = 0
    for j in range(A_HEADS // 2):
        put_heads(qa_ref, 2 * j, norm_rope(col(c), qg_ref[...]) * qscale)
        c += 1
    put_heads(ka_ref, 0, norm_rope(col(c), kg_ref[...]))
    put_values(va_ref, col(c + 1))
    c += 2
    nqb = B_HEADS // 2
    qbt_ref[...] = (z[:, c * 128:(c + nqb) * 128] * qscale).T.astype(BF16)
    c += nqb
    put_heads(kb_ref, 0, col(c))
    y = col(c + 1)
    lane = lax.broadcasted_iota(jnp.int32, y.shape, 1)
    tail = jnp.where(lane == HEAD_DIM, 1.0, 0.0)
    vbt_ref[0] = jnp.where(lane < HEAD_DIM, y, tail).T.astype(BF16)
    vbt_ref[1] = jnp.where(lane < HEAD_DIM, pltpu.roll(y, HEAD_DIM, 1), tail).T.astype(BF16)
    for j in range(2 * d // 512):
        lo, hi = n_attn + j * 512, n_attn + (j + 1) * 512
        zg = _dot(nb, w_ref[:, lo:hi]) + bg_ref[:, j * 512:(j + 1) * 512]
        gate_ref[:, j * 512:(j + 1) * 512] = (1.0 / (1.0 + jnp.exp(-zg))).astype(BF16)


def _in_proj(x2, gmix, w_bf, bg, qg, kg, cos, sa, sb, bd, seq, tm):
    t, d = x2.shape
    nt = t // tm
    ns = seq // tm
    const = lambda i: (0, 0)
    tab = pl.BlockSpec((tm, 128), lambda i: (i % ns, 0))
    head = lambda nh: pl.BlockSpec((nh, tm, HEAD_DIM), lambda i: (0, i, 0))
    wide = lambda nh: pl.BlockSpec((nh, tm, 2 * HEAD_DIM), lambda i: (0, i, 0))
    return pl.pallas_call(
        _in_proj_kernel,
        grid=(nt,),
        in_specs=[
            pl.BlockSpec((tm, d), lambda i: (i, 0)),
            pl.BlockSpec((1, d), const),
            pl.BlockSpec(w_bf.shape, const),
            pl.BlockSpec((1, 2 * d), const),
            pl.BlockSpec((1, 128), const),
            pl.BlockSpec((1, 128), const),
            tab, tab, tab,
            pl.BlockSpec((128, 128), const),
        ],
        out_specs=[head(A_HEADS), head(A_KV), wide(A_KV),
                   pl.BlockSpec((B_HEADS * HEAD_DIM, tm), lambda i: (0, i)), head(B_KV),
                   pl.BlockSpec((B_KV, 2 * HEAD_DIM, tm), lambda i: (0, 0, i)),
                   pl.BlockSpec((tm, 2 * d), lambda i: (i, 0))],
        out_shape=[
            jax.ShapeDtypeStruct((A_HEADS, t, HEAD_DIM), BF16),
            jax.ShapeDtypeStruct((A_KV, t, HEAD_DIM), BF16),
            jax.ShapeDtypeStruct((A_KV, t, 2 * HEAD_DIM), BF16),
            jax.ShapeDtypeStruct((B_HEADS * HEAD_DIM, t), BF16),
            jax.ShapeDtypeStruct((B_KV, t, HEAD_DIM), BF16),
            jax.ShapeDtypeStruct((B_KV, 2 * HEAD_DIM, t), BF16),
            jax.ShapeDtypeStruct((t, 2 * d), BF16),
        ],
        compiler_params=_cparams("parallel"),
        name="in_proj",
    )(x2, gmix, w_bf, bg, qg, kg, cos, sa, sb, bd)


def _attn_a_kernel(q_ref, k_ref, v_ref, o_ref, *, kc):
    r, tq, hd = q_ref.shape
    seq = k_ref.shape[0]
    q = q_ref[...].reshape(r * tq, hd)
    m = jnp.full((r * tq, 1), -jnp.inf, F32)
    acc = jnp.zeros((r * tq, v_ref.shape[1]), F32)
    for c in range(seq // kc):
        s = _dot_nt(q, k_ref[c * kc:(c + 1) * kc, :])
        m_new = jnp.maximum(m, jnp.max(s, axis=-1, keepdims=True))
        p = jnp.exp2(s - m_new)
        acc = jnp.exp2(m - m_new) * acc + _dot(p.astype(BF16), v_ref[c * kc:(c + 1) * kc, :])
        m = m_new
    o = acc[:, :hd] / acc[:, hd:hd + 1]
    for j in range(r):
        o_ref[:, j * hd:(j + 1) * hd] = o[j * tq:(j + 1) * tq].astype(BF16)


def _attn_a(qa, ka, va, bsz, seq, tq):
    t = qa.shape[1]
    r = A_HEADS // A_KV
    nq = seq // tq
    return pl.pallas_call(
        functools.partial(_attn_a_kernel, kc=min(512, seq)),
        grid=(bsz, A_KV, nq),
        in_specs=[
            pl.BlockSpec((r, tq, HEAD_DIM), lambda b, g, i: (g, b * nq + i, 0)),
            pl.BlockSpec((None, seq, HEAD_DIM), lambda b, g, i: (g, b, 0)),
            pl.BlockSpec((None, seq, 2 * HEAD_DIM), lambda b, g, i: (g, b, 0)),
        ],
        out_specs=pl.BlockSpec((tq, r * HEAD_DIM), lambda b, g, i: (b * nq + i, g)),
        out_shape=jax.ShapeDtypeStruct((t, A_HEADS * HEAD_DIM), BF16),
        compiler_params=_cparams("parallel", "parallel", "parallel"),
        name="attn_a",
    )(qa, ka, va)


def _attn_b_kernel(relb_ref, sink_ref, qt_ref, k_ref, vt_ref, bucket_t_ref, o_ref, bias_ref):
    b = pl.program_id(0)
    n = pl.program_id(1)
    nblk = pl.num_programs(1)
    r = B_HEADS // B_KV
    qb = Q_BLOCK
    kw = 3 * Q_BLOCK
    hd = HEAD_DIM

    @pl.when(jnp.logical_and(b == 0, n == 0))
    def _():
        for v in range(bucket_t_ref.shape[0]):
            bucket_t = bucket_t_ref[v]
            for h in range(B_HEADS):
                acc = jnp.full((kw, qb), NEG, F32)
                for k in range(N_BUCKETS):
                    acc = jnp.where(bucket_t == k, relb_ref[k, h] * LOG2E, acc)
                bias_ref[v, h // r, :, (h % r) * qb:(h % r + 1) * qb] = acc

    nq = qt_ref.shape[1] // qb
    last = nblk * nq - 1
    head_of_lane = lax.broadcasted_iota(jnp.int32, (1, r * qb), 1) // qb
    for i in range(nq):
        blk = n * nq + i
        variant = jnp.where(blk == 0, 0, jnp.where(blk == last, 2, 1))
        start = pl.multiple_of(jnp.clip(blk - 1, 0, last - 2) * qb, qb)
        for g in range(B_KV):
            kwin = k_ref[g, pl.ds(start, kw), :]
            vtw = vt_ref[g, :, pl.ds(start, kw)]
            qt = jnp.concatenate(
                [qt_ref[(g * r + j) * hd:(g * r + j + 1) * hd, i * qb:(i + 1) * qb] for j in range(r)], axis=1)
            st = _dot(kwin, qt) + bias_ref[variant, g]
            sink = jnp.zeros((1, r * qb), F32)
            for j in range(r):
                sink = jnp.where(head_of_lane == j, sink_ref[0, g * r + j] * LOG2E, sink)
            m = jnp.maximum(jnp.max(st, axis=0, keepdims=True), sink)
            ot = _dot(vtw, jnp.exp2(st - m).astype(BF16))
            ot = ot / (ot[hd:hd + 1] + jnp.exp2(sink - m))
            for j in range(r):
                h = g * r + j
                o_ref[i * qb:(i + 1) * qb, h * hd:(h + 1) * hd] = ot[:, j * qb:(j + 1) * qb].T[:, :hd].astype(BF16)


def _attn_b(qbt, kb, vbt, bucket_t, rel_bias, sink, bsz, seq):
    t = qbt.shape[1]
    nb = seq // Q_BLOCK
    assert nb >= 3, "the shifted edge windows need at least three query blocks per sequence"
    nq = 2 if nb % 2 == 0 else 1
    ns = nb // nq
    r = B_HEADS // B_KV
    return pl.pallas_call(
        _attn_b_kernel,
        grid=(bsz, ns),
        in_specs=[
            pl.BlockSpec(memory_space=pltpu.SMEM),
            pl.BlockSpec(memory_space=pltpu.SMEM),
            pl.BlockSpec((B_HEADS * HEAD_DIM, nq * Q_BLOCK), lambda b, n: (0, b * ns + n)),
            pl.BlockSpec((B_KV, seq, HEAD_DIM), lambda b, n: (0, b, 0)),
            pl.BlockSpec((B_KV, 2 * HEAD_DIM, seq), lambda b, n: (0, 0, b)),
            pl.BlockSpec(bucket_t.shape, lambda b, n: (0, 0, 0)),
        ],
        out_specs=pl.BlockSpec((nq * Q_BLOCK, B_HEADS * HEAD_DIM), lambda b, n: (b * ns + n, 0)),
        out_shape=jax.ShapeDtypeStruct((t, B_HEADS * HEAD_DIM), BF16),
        scratch_shapes=[pltpu.VMEM((bucket_t.shape[0], B_KV, 3 * Q_BLOCK, r * Q_BLOCK), F32)],
        compiler_params=_cparams("arbitrary", "arbitrary"),
        name="attn_b",
    )(rel_bias, sink, qbt, kb, vbt, bucket_t)


def _merge_part(oa_ref, ob_ref, gate_ref, x_ref, wa_ref, wb_ref, wo_ref, gffn_ref, wq_ref, sk3_ref,
                h_ref, xnt_ref, sc_ref):
    d = x_ref.shape[1]
    pa = _dot(oa_ref[...], wa_ref[...])
    pb = _dot(ob_ref[...], wb_ref[...])
    mix = gate_ref[:, :d].astype(F32) * pa + gate_ref[:, d:].astype(F32) * pb
    h = x_ref[...] + _dot(mix.astype(BF16), wo_ref[...])
    h_ref[...] = h
    xn = h * lax.rsqrt(jnp.mean(h * h, axis=-1, keepdims=True) + EPS) * gffn_ref[...]
    xnb = xn.astype(BF16)
    xnt_ref[...] = xn.T.astype(BF16)
    half = PEER_DQ // 2
    qh, ql = _split_bf16(_dot(xnb, wq_ref[...]))
    for hc in range(2 * PEER_HEADS):
        cols = slice(hc * half, (hc + 1) * half)
        q3 = jnp.concatenate([qh[:, cols], ql[:, cols], qh[:, cols]], axis=1)
        sc_ref[hc] = _dot_nt(sk3_ref[hc], q3)


def _pop_max(x, order, exact):
    m = jnp.max(x, axis=0, keepdims=True)
    if exact:
        first = jnp.min(jnp.where(x == m, order, jnp.inf), axis=0, keepdims=True)
        return m, order == first
    return m, x == m


def _topk_ranks(s):
    rows = lax.broadcasted_iota(jnp.int32, s.shape, 0).astype(F32)
    rank = jnp.full(s.shape, float(PEER_TOPK), F32)
    x = s
    vals = []
    for a in range(PEER_TOPK):
        m, hit = _pop_max(x, rows, True)
        rank = jnp.where(hit, float(a), rank)
        x = jnp.where(hit, -jnp.inf, x)
        vals.append(m)
    return vals, rank


def _batcher_layers(n):
    layers = []
    p = 1
    while p < n:
        k = p
        while k >= 1:
            layer = []
            for j in range(k % p, n - k, 2 * k):
                for i in range(min(k, n - j - k)):
                    if (i + j) // (2 * p) == (i + j + k) // (2 * p):
                        layer.append((i + j, i + j + k))
            layers.append(layer)
            k //= 2
        p *= 2
    return layers


_SORT16 = _batcher_layers(PEER_TOPK)


def _exchange(xs, i, j):
    xs[i], xs[j] = jnp.maximum(xs[i], xs[j]), jnp.minimum(xs[i], xs[j])


def _sorted_top16(xs):
    xs = list(xs)
    n = len(xs)
    for layer in _SORT16:
        for i, j in layer:
            _exchange(xs, i, j)
    for shift in (4, 2, 1):
        other = [pltpu.roll(x, shift, 0) for x in xs]
        xs = [jnp.maximum(xs[p], other[n - 1 - p]) for p in range(n)]
        stride = n // 2
        while stride >= 1:
            for base in range(0, n, 2 * stride):
                for i in range(base, base + stride):
                    _exchange(xs, i, i + stride)
            stride //= 2
    return xs


def _merge_select_kernel(oa_ref, ob_ref, gate_ref, x_ref, wa_ref, wb_ref, wo_ref, gffn_ref, wq_ref, sk3_ref,
                         flat_ref, arow_ref, rowsel_ref,
                         h_ref, xnt_ref, c1_ref, lim_ref, p2_ref, r2_ref, sc_ref, cand_ref):
    step = pl.program_id(0)
    slot = lax.rem(step, 2)
    sc_new = sc_ref.at[slot]
    sc_ref = sc_ref.at[1 - slot]
    tb = V7X_LANES
    flat = flat_ref[...]
    arow = arow_ref[...]
    nv = PEER_NKEYS // 8
    k16 = float(PEER_TOPK)

    @pl.when(step == 0)
    def _():
        key = lax.broadcasted_iota(jnp.int32, sc_ref.shape, 1).astype(F32)
        odd = lax.rem(lax.broadcasted_iota(jnp.int32, sc_ref.shape, 0), 2) == 1
        sc_ref[...] = -key * jnp.where(odd, 0.618034, 1.0)
        cand_ref[...] = jnp.full(cand_ref.shape, -jnp.inf, F32)

    _merge_part(oa_ref, ob_ref, gate_ref, x_ref, wa_ref, wb_ref, wo_ref, gffn_ref, wq_ref, sk3_ref,
                h_ref, xnt_ref, sc_new)

    def head_fast(h, cols, cand):
        s1 = [sc_ref[2 * h, 8 * k:8 * k + 8, cols] for k in range(nv)]
        s2 = [sc_ref[2 * h + 1, 8 * k:8 * k + 8, cols] for k in range(nv)]
        v1 = _sorted_top16(s1)
        v2 = _sorted_top16(s2)
        r1 = [v[0:1] for v in v1]
        r2 = [v[0:1] for v in v2]
        for i, (a, b) in enumerate(_CAND):
            cand[i:i + 1, :] = r1[a] + r2[b]
        top = r1[0] + r2[0]
        sums = cand[...]
        x = sums
        for _ in range(PEER_TOPK):
            m, hit = _pop_max(x, flat, False)
            x = jnp.where(hit, -jnp.inf, x)
        sel = sums >= m
        z = jnp.sum(jnp.where(sel, jnp.exp(sums - top), 0.0), axis=0, keepdims=True)
        counts = _dot(rowsel_ref[...], jnp.where(sel, 1.0, 0.0).astype(BF16))
        bmax = [jnp.broadcast_to(counts[a:a + 1], (8, tb)) for a in range(PEER_TOPK)]
        lim, rank2 = [], []
        for k in range(nv):
            l = jnp.zeros((8, tb), F32)
            r = jnp.full((8, tb), k16, F32)
            for a in range(PEER_TOPK - 1, -1, -1):
                l = jnp.where(s1[k] >= v1[a], bmax[a], l)
                r = jnp.where(s2[k] >= v2[a], float(a), r)
            lim.append(l)
            rank2.append(r)
        c1_ref[h, :, cols] = 0.5 * jnp.exp(jnp.concatenate(s1, axis=0) - r1[0]) / z
        lim_ref[h, :, cols] = jnp.concatenate(lim, axis=0)
        p2_ref[h, :, cols] = jnp.exp(jnp.concatenate(s2, axis=0) - r2[0]).astype(BF16)
        r2_ref[h, :, cols] = jnp.concatenate(rank2, axis=0).astype(BF16)
        tied = jnp.sum(counts, axis=0, keepdims=True) != k16
        for s, v, r in ((s1, v1, r1), (s2, v2, r2)):
            count = sum(jnp.where(x >= v[-1], 1.0, 0.0) for x in s)
            tied = tied | (jnp.sum(count, axis=0, keepdims=True) != k16)
            for a in range(PEER_TOPK - 1):
                tied = tied | (r[a] == r[a + 1])
        return tied

    def head_exact(h, cols, cand):
        s1 = sc_ref[2 * h, :, cols]
        s2 = sc_ref[2 * h + 1, :, cols]
        v1, rank1 = _topk_ranks(s1)
        v2, rank2 = _topk_ranks(s2)
        for i, (a, b) in enumerate(_CAND):
            cand[i:i + 1, :] = v1[a] + v2[b]
        top = v1[0] + v2[0]
        sel = jnp.zeros(cand.shape, F32)
        z = jnp.zeros((1, tb), F32)
        x = cand[...]
        for _ in range(PEER_TOPK):
            m, hit = _pop_max(x, flat, True)
            sel = jnp.where(hit, 1.0, sel)
            x = jnp.where(hit, -jnp.inf, x)
            z = z + jnp.exp(m - top)
        lim = jnp.zeros(s1.shape, F32)
        for a in range(PEER_TOPK):
            bmax = jnp.sum(jnp.where(arow == float(a), sel, 0.0), axis=0, keepdims=True)
            lim = jnp.where(rank1 == float(a), bmax, lim)
        c1_ref[h, :, cols] = 0.5 * jnp.exp(s1 - v1[0]) / z
        lim_ref[h, :, cols] = lim
        p2_ref[h, :, cols] = jnp.exp(s2 - v2[0]).astype(BF16)
        r2_ref[h, :, cols] = rank2.astype(BF16)

    blocks = [slice(j * tb, (j + 1) * tb) for j in range(sc_ref.shape[2] // tb)]
    tied = None
    for j, cols in enumerate(blocks):
        for h in range(PEER_HEADS):
            t = head_fast(h, cols, cand_ref.at[j * PEER_HEADS + h])
            tied = t if tied is None else tied | t

    @pl.when(jnp.max(jnp.where(tied, 1.0, 0.0)) > 0.0)
    def _():
        for cols in blocks:
            def one_head(h, carry, cols=cols):
                head_exact(h, cols, cand_ref.at[0])
                return carry

            lax.fori_loop(0, PEER_HEADS, one_head, 0)


def _merge_select(oa, ob, gate, x2, wa, wb, wo, gffn, wq, sk3, tm):
    t, d = x2.shape
    nt = t // tm
    nsc = 2 * PEER_HEADS
    nk = PEER_NKEYS
    flat = np.full((_NCAND_PAD, 1), 1023.0, np.float32)
    arow = np.full((_NCAND_PAD, 1), -1.0, np.float32)
    for i, (a, b) in enumerate(_CAND):
        flat[i, 0] = a * PEER_TOPK + b
        arow[i, 0] = a
    rowsel = jnp.asarray(arow.T == np.arange(PEER_TOPK, dtype=np.float32)[:, None]).astype(BF16)
    flat = jnp.asarray(np.broadcast_to(flat, (_NCAND_PAD, V7X_LANES)))
    arow = jnp.asarray(np.broadcast_to(arow, (_NCAND_PAD, V7X_LANES)))
    cur = lambda i: jnp.minimum(i, nt - 1)
    prev = lambda i: jnp.maximum(i - 1, 0)
    row = lambda w: pl.BlockSpec((tm, w), lambda i: (cur(i), 0))
    const = lambda a: pl.BlockSpec(a.shape, lambda i: (0,) * a.ndim, pipeline_mode=pl.Buffered(1))
    sel_out = pl.BlockSpec((PEER_HEADS, nk, tm), lambda i: (0, 0, prev(i)))
    sel_shape = lambda dt: jax.ShapeDtypeStruct((PEER_HEADS, nk, t), dt)
    return pl.pallas_call(
        _merge_select_kernel,
        grid=(nt + 1,),
        in_specs=[row(oa.shape[1]), row(ob.shape[1]), row(2 * d), row(d),
                  const(wa), const(wb), const(wo), const(gffn), const(wq), const(sk3),
                  const(flat), const(arow), const(rowsel)],
        out_specs=[row(d), pl.BlockSpec((d, tm), lambda i: (0, cur(i))), sel_out, sel_out, sel_out, sel_out],
        out_shape=[jax.ShapeDtypeStruct((t, d), F32), jax.ShapeDtypeStruct((d, t), BF16),
                   sel_shape(F32), sel_shape(F32), sel_shape(BF16), sel_shape(BF16)],
        scratch_shapes=[pltpu.VMEM((2, nsc, nk, tm), F32),
                        pltpu.VMEM((PEER_HEADS * (tm // V7X_LANES), _NCAND_PAD, V7X_LANES), F32)],
        compiler_params=_cparams("arbitrary"),
        name="merge_select",
    )(oa, ob, gate, x2, wa, wb, wo, gffn, wq, sk3, flat, arow, rowsel)


def _experts_kernel(xnt_ref, u_ref, vt_ref, c1_ref, lim_ref, p2_ref, r2_ref, h_ref, gfin_ref, o_ref,
                    acc_ref, g_ref, *, nblk):
    s = pl.program_id(0)
    e = lax.rem(s, nblk)
    rows = c1_ref.shape[1]
    nk = PEER_NKEYS

    slot = lax.rem(s, 2)
    g_new = g_ref.at[slot]
    g_old = g_ref.at[1 - slot]

    @pl.when(s == 0)
    def _():
        acc_ref[...] = jnp.zeros_like(acc_ref)
        g_old[...] = jnp.zeros_like(g_old)

    def row_bf16(ref, h, r):
        x = jnp.broadcast_to(ref[h, r:r + 1, :], (16, ref.shape[2])).astype(BF16)
        return jnp.concatenate([x] * (nk // 16), axis=0)

    mc = min(512, u_ref.shape[0])
    a = jnp.concatenate([_dot(u_ref[i:i + mc, :], xnt_ref[...]) for i in range(0, u_ref.shape[0], mc)], axis=0)
    contrib = _dot(vt_ref[...], g_old[...])
    acc_ref[...] = jnp.where(lax.rem(s + nblk - 1, nblk) == 0, contrib, acc_ref[...] + contrib)
    for r in range(rows):
        w = None
        for h in range(PEER_HEADS):
            term = jnp.where(r2_ref[h] < row_bf16(lim_ref, h, r), p2_ref[h], jnp.zeros((), BF16)) \
                * row_bf16(c1_ref, h, r)
            w = term if w is None else w + term
        ab = a[r * nk:(r + 1) * nk].astype(BF16)
        g_new[r * nk:(r + 1) * nk, :] = ab * (1.0 + lax.erf(ab * INV_SQRT2)) * w

    @pl.when(jnp.logical_and(e == 0, s > 0))
    def _():
        hh = h_ref[...] + acc_ref[...].T
        o_ref[...] = hh * lax.rsqrt(jnp.mean(hh * hh, axis=-1, keepdims=True) + EPS) * gfin_ref[...]


def _experts(xnt, u_bf, vt_blk, c1, lim, p2, r2, h, gfin, tt):
    t, d = h.shape
    nblk, _, et = vt_blk.shape
    rows = et // PEER_NKEYS
    nt = t // tt
    tile = lambda s: jnp.minimum(s // nblk, nt - 1)
    blk = lambda s: lax.rem(s, nblk)
    prev_tile = lambda s: jnp.maximum(s - 1, 0) // nblk
    prev_blk = lambda s: lax.rem(s + nblk - 1, nblk)
    tok3 = pl.BlockSpec((PEER_HEADS, PEER_NKEYS, tt), lambda s: (0, 0, tile(s)))
    row3 = pl.BlockSpec((PEER_HEADS, rows, tt), lambda s: (0, blk(s), tile(s)))
    return pl.pallas_call(
        functools.partial(_experts_kernel, nblk=nblk),
        grid=(nt * nblk + 1,),
        in_specs=[
            pl.BlockSpec((d, tt), lambda s: (0, tile(s))),
            pl.BlockSpec((et, d), lambda s: (blk(s), 0)),
            pl.BlockSpec((None, d, et), lambda s: (prev_blk(s), 0, 0)),
            row3, row3, tok3, tok3,
            pl.BlockSpec((tt, d), lambda s: (prev_tile(s), 0)),
            pl.BlockSpec((1, d), lambda s: (0, 0)),
        ],
        out_specs=pl.BlockSpec((tt, d), lambda s: (prev_tile(s), 0)),
        out_shape=jax.ShapeDtypeStruct((t, d), F32),
        scratch_shapes=[pltpu.VMEM((d, tt), F32), pltpu.VMEM((2, et, tt), BF16)],
        compiler_params=_cparams("arbitrary"),
        name="experts",
    )(xnt, u_bf, vt_blk, c1, lim, p2, r2, h, gfin)


def _rope_tables(seq):
    rows = seq // GRID_W
    row = np.broadcast_to(np.arange(rows)[:, None], (rows, GRID_W)).reshape(-1)
    col = np.broadcast_to(np.arange(GRID_W)[None, :], (rows, GRID_W)).reshape(-1)
    half = HEAD_DIM // 4
    freqs = ROPE_THETA ** (-np.arange(half, dtype=np.float64) / half)

    def cs(pos):
        ang = pos.astype(np.float64)[:, None] * freqs[None, :]
        return np.cos(ang), np.sin(ang)

    cr, sr = cs(row)
    cc, sc = cs(col)
    zero = np.zeros_like(sr)
    cos = np.concatenate([cr, cr, cc, cc], axis=1)
    sa = np.concatenate([-sr, zero, -sc, zero], axis=1)
    sb = np.concatenate([zero, sr, zero, sc], axis=1)
    tile = lambda a: jnp.asarray(np.concatenate([a, a], axis=1).astype(np.float32))
    return tile(cos), tile(sa), tile(sb)


def _bucket_tables():
    qi = np.arange(Q_BLOCK)
    kj = np.arange(3 * Q_BLOCK)
    rel = kj[None, :, None] - Q_BLOCK * np.arange(3)[:, None, None] - qi[None, None, :]
    nbk = N_BUCKETS // 2
    max_exact = nbk // 2
    assert (max_exact, MAX_DISTANCE, nbk - max_exact) == (8, 128, 8), "the integer form below assumes these constants"
    n = np.abs(rel)
    log_part = np.floor(np.log2(np.maximum(n * n // 64, 1))).astype(np.int64)
    log_part = np.where(2 ** (log_part + 1) * 64 <= n * n, log_part + 1, log_part)
    log_part = np.where(2 ** log_part * 64 > n * n, log_part - 1, log_part)
    large = np.minimum(max_exact + log_part, nbk - 1)
    bucket = np.where(rel > 0, nbk, 0) + np.where(n < max_exact, n, large)
    return jnp.asarray(np.where(n <= WINDOW, bucket, N_BUCKETS).astype(np.int32))


def _tile_sizes(seq, tokens):
    tm = min(512, seq)
    tq = min(512, seq)
    ts = min(256, tokens)
    tt = min(512, tokens)
    return tm, tq, ts, tt


def kernel(x, g_mix, w_in, b_gate, q_norm_g, k_norm_g, rel_bias, sink, w_branch_a, w_branch_b, w_out, g_ffn,
           peer_wq, peer_subkeys, peer_u, peer_v, g_final):
    bsz, seq, d = x.shape
    t = bsz * seq
    depth = w_in.shape[0]
    tm, tq, ts, tt = _tile_sizes(seq, t)
    assert depth == 1, "the experts stage fuses the final norm, so only a single layer is supported"
    assert seq % tm == 0 and seq % Q_BLOCK == 0 and t % ts == 0 and t % tt == 0

    cos, sa, sb = _rope_tables(seq)
    bucket_t = _bucket_tables()
    bd = jnp.asarray(np.kron(np.eye(2, dtype=np.float32),
                             np.full((HEAD_DIM, HEAD_DIM), 1.0 / HEAD_DIM, np.float32))).astype(BF16)
    expert_block = 2048

    h = x.reshape(t, d)
    for l in range(depth):
        qg = jnp.tile(q_norm_g[l], 2)[None, :]
        kg = jnp.tile(k_norm_g[l], 2)[None, :]
        qa, ka, va, qbt, kb, vbt, gate = _in_proj(
            h, g_mix[l][None, :], w_in[l].astype(BF16), b_gate[l][None, :], qg, kg, cos, sa, sb, bd, seq, tm)
        oa = _attn_a(qa, ka, va, bsz, seq, tq)
        ob = _attn_b(qbt, kb, vbt, bucket_t, rel_bias, sink[l][None, :], bsz, seq)
        skh, skl = _split_bf16(peer_subkeys[l].reshape(2 * PEER_HEADS, PEER_NKEYS, PEER_DQ // 2))
        sk3 = jnp.concatenate([skh, skh, skl], axis=-1)
        h1, xnt, c1, lim, p2, r2 = _merge_select(
            oa, ob, gate, h, w_branch_a[l].astype(BF16), w_branch_b[l].astype(BF16), w_out[l].astype(BF16),
            g_ffn[l][None, :], peer_wq[l].astype(BF16), sk3, ts)
        vt_blk = peer_v[l].astype(BF16).reshape(-1, expert_block, d).transpose(0, 2, 1)
        out = _experts(xnt, peer_u[l].astype(BF16), vt_blk, c1, lim, p2, r2, h1, g_final[None, :], tt)
    return out.reshape(bsz, seq, d)
```

```python
import functools

import numpy as np
import jax
import jax.numpy as jnp
from jax import lax
from jax.experimental import pallas as pl
from jax.experimental.pallas import tpu as pltpu

F32 = jnp.float32
BF16 = jnp.bfloat16

HEAD_DIM = 64
A_HEADS, A_KV = 8, 2
B_HEADS, B_KV = 8, 2
Q_BLOCK = 128
WINDOW = 128
GRID_W = 64
ROPE_THETA = 10000.0
N_BUCKETS = 32
MAX_DISTANCE = 128
PEER_HEADS = 8
PEER_NKEYS = 128
PEER_TOPK = 16
PEER_DQ = 256
EPS = 1e-6
NEG = -1e30
INV_SQRT2 = 0.7071067811865476
LOG2E = 1.4426950408889634

V7X_LANES = 128
V7X_VMEM_LIMIT_BYTES = 56 * 1024 * 1024

_CAND = [(a, b) for a in range(PEER_TOPK) for b in range(PEER_TOPK) if (a + 1) * (b + 1) <= PEER_TOPK]
_NCAND = len(_CAND)
_NCAND_PAD = -(-_NCAND // 8) * 8


def _cparams(*sem):
    return pltpu.CompilerParams(dimension_semantics=sem, vmem_limit_bytes=V7X_VMEM_LIMIT_BYTES)


def _split_bf16(x):
    hi = x.astype(BF16)
    lo = (x - hi.astype(F32)).astype(BF16)
    return hi, lo


def _dot(a, b):
    return jnp.dot(a, b, preferred_element_type=F32)


def _dot_nt(a, b):
    return lax.dot_general(a, b, (((1,), (1,)), ((), ())), preferred_element_type=F32)


def _in_proj_kernel(x_ref, gmix_ref, w_ref, bg_ref, qg_ref, kg_ref, cos_ref, sa_ref, sb_ref, bd_ref,
                    qa_ref, ka_ref, va_ref, qbt_ref, kb_ref, vbt_ref, gate_ref):
    d = x_ref.shape[1]
    x = x_ref[...]
    n = x * lax.rsqrt(jnp.mean(x * x, axis=-1, keepdims=True) + EPS) * gmix_ref[...]
    nb = n.astype(BF16)
    cos, sa, sb, bd = cos_ref[...], sa_ref[...], sb_ref[...], bd_ref[...]

    def norm_rope(q, gain):
        hi, lo = _split_bf16(q * q)
        ms = _dot(hi, bd) + _dot(lo, bd)
        y = q * lax.rsqrt(ms + EPS) * gain
        return y * cos + pltpu.roll(y, V7X_LANES - 16, 1) * sa + pltpu.roll(y, 16, 1) * sb

    def put_heads(ref, first, y):
        ref[first] = y[:, :HEAD_DIM].astype(BF16)
        ref[first + 1] = y[:, HEAD_DIM:].astype(BF16)

    def put_values(ref, y):
        lane = lax.broadcasted_iota(jnp.int32, y.shape, 1)
        tail = jnp.where(lane == HEAD_DIM, 1.0, 0.0)
        ref[0] = jnp.where(lane < HEAD_DIM, y, tail).astype(BF16)
        ref[1] = jnp.where(lane < HEAD_DIM, pltpu.roll(y, HEAD_DIM, 1), tail).astype(BF16)

    qscale = HEAD_DIM ** -0.5 * LOG2E
    n_attn = (A_HEADS + 2 * A_KV + B_HEADS + 2 * B_KV) * HEAD_DIM
    z = _dot(nb, w_ref[:, :n_attn])
    pw = 2 * HEAD_DIM
    col = lambda c: z[:, c * pw:(c + 1) * pw]
    c = 0
    for j in range(A_HEADS // 2):
        put_heads(qa_ref, 2 * j, norm_rope(col(c), qg_ref[...]) * qscale)
        c += 1
    put_heads(ka_ref, 0, norm_rope(col(c), kg_ref[...]))
    put_values(va_ref, col(c + 1))
    c += 2
    nqb = B_HEADS // 2
    qbt_ref[...] = (z[:, c * pw:(c + nqb) * pw] * qscale).T.astype(BF16)
    c += nqb
    put_heads(kb_ref, 0, col(c))
    y = col(c + 1)
    lane = lax.broadcasted_iota(jnp.int32, y.shape, 1)
    tail = jnp.where(lane == HEAD_DIM, 1.0, 0.0)
    vbt_ref[0] = jnp.where(lane < HEAD_DIM, y, tail).T.astype(BF16)
    vbt_ref[1] = jnp.where(lane < HEAD_DIM, pltpu.roll(y, HEAD_DIM, 1), tail).T.astype(BF16)
    for j in range(2 * d // 512):
        lo, hi = n_attn + j * 512, n_attn + (j + 1) * 512
        zg = _dot(nb, w_ref[:, lo:hi]) + bg_ref[:, j * 512:(j + 1) * 512]
        gate_ref[:, j * 512:(j + 1) * 512] = (1.0 / (1.0 + jnp.exp(-zg))).astype(BF16)


def _in_proj(x2, gmix, w_bf, bg, qg, kg, cos, sa, sb, bd, seq, tm):
    t, d = x2.shape
    nt = t // tm
    ns = seq // tm
    const = lambda i: (0, 0)
    pw = 2 * HEAD_DIM
    tab = pl.BlockSpec((tm, pw), lambda i: (i % ns, 0))
    head = lambda nh: pl.BlockSpec((nh, tm, HEAD_DIM), lambda i: (0, i, 0))
    wide = lambda nh: pl.BlockSpec((nh, tm, 2 * HEAD_DIM), lambda i: (0, i, 0))
    return pl.pallas_call(
        _in_proj_kernel,
        grid=(nt,),
        in_specs=[
            pl.BlockSpec((tm, d), lambda i: (i, 0)),
            pl.BlockSpec((1, d), const),
            pl.BlockSpec(w_bf.shape, const),
            pl.BlockSpec((1, 2 * d), const),
            pl.BlockSpec((1, pw), const),
            pl.BlockSpec((1, pw), const),
            tab, tab, tab,
            pl.BlockSpec((pw, pw), const),
        ],
        out_specs=[head(A_HEADS), head(A_KV), wide(A_KV),
                   pl.BlockSpec((B_HEADS * HEAD_DIM, tm), lambda i: (0, i)), head(B_KV),
                   pl.BlockSpec((B_KV, 2 * HEAD_DIM, tm), lambda i: (0, 0, i)),
                   pl.BlockSpec((tm, 2 * d), lambda i: (i, 0))],
        out_shape=[
            jax.ShapeDtypeStruct((A_HEADS, t, HEAD_DIM), BF16),
            jax.ShapeDtypeStruct((A_KV, t, HEAD_DIM), BF16),
            jax.ShapeDtypeStruct((A_KV, t, 2 * HEAD_DIM), BF16),
            jax.ShapeDtypeStruct((B_HEADS * HEAD_DIM, t), BF16),
            jax.ShapeDtypeStruct((B_KV, t, HEAD_DIM), BF16),
            jax.ShapeDtypeStruct((B_KV, 2 * HEAD_DIM, t), BF16),
            jax.ShapeDtypeStruct((t, 2 * d), BF16),
        ],
        compiler_params=_cparams("parallel"),
        name="in_proj",
    )(x2, gmix, w_bf, bg, qg, kg, cos, sa, sb, bd)


def _attn_a_kernel(q_ref, k_ref, v_ref, o_ref, *, kc):
    r, tq, hd = q_ref.shape
    seq = k_ref.shape[0]
    q = q_ref[...].reshape(r * tq, hd)
    m = jnp.full((r * tq, 1), -jnp.inf, F32)
    acc = jnp.zeros((r * tq, v_ref.shape[1]), F32)
    for c in range(seq // kc):
        s = _dot_nt(q, k_ref[c * kc:(c + 1) * kc, :])
        m_new = jnp.maximum(m, jnp.max(s, axis=-1, keepdims=True))
        p = jnp.exp2(s - m_new)
        acc = jnp.exp2(m - m_new) * acc + _dot(p.astype(BF16), v_ref[c * kc:(c + 1) * kc, :])
        m = m_new
    o = acc[:, :hd] / acc[:, hd:hd + 1]
    for j in range(r):
        o_ref[:, j * hd:(j + 1) * hd] = o[j * tq:(j + 1) * tq].astype(BF16)


def _attn_a(qa, ka, va, bsz, seq, tq):
    t = qa.shape[1]
    r = A_HEADS // A_KV
    nq = seq // tq
    return pl.pallas_call(
        functools.partial(_attn_a_kernel, kc=min(512, seq)),
        grid=(bsz, A_KV, nq),
        in_specs=[
            pl.BlockSpec((r, tq, HEAD_DIM), lambda b, g, i: (g, b * nq + i, 0)),
            pl.BlockSpec((None, seq, HEAD_DIM), lambda b, g, i: (g, b, 0)),
            pl.BlockSpec((None, seq, 2 * HEAD_DIM), lambda b, g, i: (g, b, 0)),
        ],
        out_specs=pl.BlockSpec((tq, r * HEAD_DIM), lambda b, g, i: (b * nq + i, g)),
        out_shape=jax.ShapeDtypeStruct((t, A_HEADS * HEAD_DIM), BF16),
        compiler_params=_cparams("parallel", "parallel", "parallel"),
        name="attn_a",
    )(qa, ka, va)


def _attn_b_kernel(relb_ref, sink_ref, qt_ref, k_ref, vt_ref, bucket_t_ref, o_ref, bias_ref):
    b = pl.program_id(0)
    n = pl.program_id(1)
    nblk = pl.num_programs(1)
    r = B_HEADS // B_KV
    qb = Q_BLOCK
    kw = 3 * Q_BLOCK
    hd = HEAD_DIM

    @pl.when(jnp.logical_and(b == 0, n == 0))
    def _():
        for v in range(bucket_t_ref.shape[0]):
            bucket_t = bucket_t_ref[v]
            for h in range(B_HEADS):
                acc = jnp.full((kw, qb), NEG, F32)
                for k in range(N_BUCKETS):
                    acc = jnp.where(bucket_t == k, relb_ref[k, h] * LOG2E, acc)
                bias_ref[v, h // r, :, (h % r) * qb:(h % r + 1) * qb] = acc

    nq = qt_ref.shape[1] // qb
    last = nblk * nq - 1
    head_of_lane = lax.broadcasted_iota(jnp.int32, (1, r * qb), 1) // qb
    for i in range(nq):
        blk = n * nq + i
        variant = jnp.where(blk == 0, 0, jnp.where(blk == last, 2, 1))
        start = pl.multiple_of(jnp.clip(blk - 1, 0, last - 2) * qb, qb)
        for g in range(B_KV):
            kwin = k_ref[g, pl.ds(start, kw), :]
            vtw = vt_ref[g, :, pl.ds(start, kw)]
            qt = jnp.concatenate(
                [qt_ref[(g * r + j) * hd:(g * r + j + 1) * hd, i * qb:(i + 1) * qb] for j in range(r)], axis=1)
            st = _dot(kwin, qt) + bias_ref[variant, g]
            sink = jnp.zeros((1, r * qb), F32)
            for j in range(r):
                sink = jnp.where(head_of_lane == j, sink_ref[0, g * r + j] * LOG2E, sink)
            m = jnp.maximum(jnp.max(st, axis=0, keepdims=True), sink)
            ot = _dot(vtw, jnp.exp2(st - m).astype(BF16))
            ot = ot / (ot[hd:hd + 1] + jnp.exp2(sink - m))
            for j in range(r):
                h = g * r + j
                o_ref[i * qb:(i + 1) * qb, h * hd:(h + 1) * hd] = ot[:, j * qb:(j + 1) * qb].T[:, :hd].astype(BF16)


def _attn_b(qbt, kb, vbt, bucket_t, rel_bias, sink, bsz, seq):
    t = qbt.shape[1]
    nb = seq // Q_BLOCK
    assert nb >= 3, "the shifted edge windows need at least three query blocks per sequence"
    nq = 2 if nb % 2 == 0 else 1
    ns = nb // nq
    r = B_HEADS // B_KV
    return pl.pallas_call(
        _attn_b_kernel,
        grid=(bsz, ns),
        in_specs=[
            pl.BlockSpec(memory_space=pltpu.SMEM),
            pl.BlockSpec(memory_space=pltpu.SMEM),
            pl.BlockSpec((B_HEADS * HEAD_DIM, nq * Q_BLOCK), lambda b, n: (0, b * ns + n)),
            pl.BlockSpec((B_KV, seq, HEAD_DIM), lambda b, n: (0, b, 0)),
            pl.BlockSpec((B_KV, 2 * HEAD_DIM, seq), lambda b, n: (0, 0, b)),
            pl.BlockSpec(bucket_t.shape, lambda b, n: (0, 0, 0)),
        ],
        out_specs=pl.BlockSpec((nq * Q_BLOCK, B_HEADS * HEAD_DIM), lambda b, n: (b * ns + n, 0)),
        out_shape=jax.ShapeDtypeStruct((t, B_HEADS * HEAD_DIM), BF16),
        scratch_shapes=[pltpu.VMEM((bucket_t.shape[0], B_KV, 3 * Q_BLOCK, r * Q_BLOCK), F32)],
        compiler_params=_cparams("arbitrary", "arbitrary"),
        name="attn_b",
    )(rel_bias, sink, qbt, kb, vbt, bucket_t)


def _merge_kernel(oa_ref, ob_ref, gate_ref, x_ref, wa_ref, wb_ref, wo_ref, gffn_ref, wq_ref, sk3_ref,
                  h_ref, xnt_ref, sc_ref):
    d = x_ref.shape[1]
    pa = _dot(oa_ref[...], wa_ref[...])
    pb = _dot(ob_ref[...], wb_ref[...])
    mix = gate_ref[:, :d].astype(F32) * pa + gate_ref[:, d:].astype(F32) * pb
    h = x_ref[...] + _dot(mix.astype(BF16), wo_ref[...])
    h_ref[...] = h
    xn = h * lax.rsqrt(jnp.mean(h * h, axis=-1, keepdims=True) + EPS) * gffn_ref[...]
    xnb = xn.astype(BF16)
    xnt_ref[...] = xn.T.astype(BF16)
    half = PEER_DQ // 2
    qh, ql = _split_bf16(_dot(xnb, wq_ref[...]))
    for hc in range(2 * PEER_HEADS):
        cols = slice(hc * half, (hc + 1) * half)
        q3 = jnp.concatenate([qh[:, cols], ql[:, cols], qh[:, cols]], axis=1)
        sc_ref[hc] = _dot_nt(sk3_ref[hc], q3)


def _merge(oa, ob, gate, x2, wa, wb, wo, gffn, wq, sk3, tm):
    t, d = x2.shape
    const2 = lambda i: (0, 0)
    const3 = lambda i: (0, 0, 0)
    row = lambda w: pl.BlockSpec((tm, w), lambda i: (i, 0))
    nsc = 2 * PEER_HEADS
    return pl.pallas_call(
        _merge_kernel,
        grid=(t // tm,),
        in_specs=[
            row(oa.shape[1]), row(ob.shape[1]), row(2 * d), row(d),
            pl.BlockSpec(wa.shape, const2), pl.BlockSpec(wb.shape, const2), pl.BlockSpec(wo.shape, const2),
            pl.BlockSpec((1, d), const2), pl.BlockSpec(wq.shape, const2),
            pl.BlockSpec(sk3.shape, const3),
        ],
        out_specs=[row(d), pl.BlockSpec((d, tm), lambda i: (0, i)),
                   pl.BlockSpec((nsc, PEER_NKEYS, tm), lambda i: (0, 0, i))],
        out_shape=[
            jax.ShapeDtypeStruct((t, d), F32),
            jax.ShapeDtypeStruct((d, t), BF16),
            jax.ShapeDtypeStruct((nsc, PEER_NKEYS, t), F32),
        ],
        compiler_params=_cparams("parallel"),
        name="merge",
    )(oa, ob, gate, x2, wa, wb, wo, gffn, wq, sk3)


def _pop_max(x, order, exact):
    m = jnp.max(x, axis=0, keepdims=True)
    if exact:
        first = jnp.min(jnp.where(x == m, order, jnp.inf), axis=0, keepdims=True)
        return m, order == first
    return m, x == m


def _topk_ranks(s):
    rows = lax.broadcasted_iota(jnp.int32, s.shape, 0).astype(F32)
    rank = jnp.full(s.shape, float(PEER_TOPK), F32)
    x = s
    vals = []
    for a in range(PEER_TOPK):
        m, hit = _pop_max(x, rows, True)
        rank = jnp.where(hit, float(a), rank)
        x = jnp.where(hit, -jnp.inf, x)
        vals.append(m)
    return vals, rank


def _batcher_layers(n):
    layers = []
    p = 1
    while p < n:
        k = p
        while k >= 1:
            layer = []
            for j in range(k % p, n - k, 2 * k):
                for i in range(min(k, n - j - k)):
                    if (i + j) // (2 * p) == (i + j + k) // (2 * p):
                        layer.append((i + j, i + j + k))
            layers.append(layer)
            k //= 2
        p *= 2
    return layers


_SORT16 = _batcher_layers(PEER_TOPK)


def _exchange(xs, i, j):
    xs[i], xs[j] = jnp.maximum(xs[i], xs[j]), jnp.minimum(xs[i], xs[j])


def _sorted_top16(xs):
    xs = list(xs)
    n = len(xs)
    for layer in _SORT16:
        for i, j in layer:
            _exchange(xs, i, j)
    for shift in (4, 2, 1):
        other = [pltpu.roll(x, shift, 0) for x in xs]
        xs = [jnp.maximum(xs[p], other[n - 1 - p]) for p in range(n)]
        stride = n // 2
        while stride >= 1:
            for base in range(0, n, 2 * stride):
                for i in range(base, base + stride):
                    _exchange(xs, i, i + stride)
            stride //= 2
    return xs


def _select_kernel(sc_ref, flat_ref, arow_ref, rowsel_ref, c1_ref, lim_ref, p2_ref, r2_ref, cand_ref):
    tb = sc_ref.shape[2]
    flat = flat_ref[...]
    arow = arow_ref[...]
    cand_ref[...] = jnp.full(cand_ref.shape, -jnp.inf, F32)
    nv = PEER_NKEYS // 8
    k16 = float(PEER_TOPK)

    def head_fast(h, cand):
        s1 = [sc_ref[2 * h, 8 * k:8 * k + 8, :] for k in range(nv)]
        s2 = [sc_ref[2 * h + 1, 8 * k:8 * k + 8, :] for k in range(nv)]
        v1 = _sorted_top16(s1)
        v2 = _sorted_top16(s2)
        r1 = [v[0:1] for v in v1]
        r2 = [v[0:1] for v in v2]
        for i, (a, b) in enumerate(_CAND):
            cand[i:i + 1, :] = r1[a] + r2[b]
        top = r1[0] + r2[0]
        sums = cand[...]
        x = sums
        for _ in range(PEER_TOPK):
            m, hit = _pop_max(x, flat, False)
            x = jnp.where(hit, -jnp.inf, x)
        sel = sums >= m
        z = jnp.sum(jnp.where(sel, jnp.exp(sums - top), 0.0), axis=0, keepdims=True)
        counts = _dot(rowsel_ref[...], jnp.where(sel, 1.0, 0.0).astype(BF16))
        bmax = [jnp.broadcast_to(counts[a:a + 1], (8, tb)) for a in range(PEER_TOPK)]
        lim, rank2 = [], []
        for k in range(nv):
            l = jnp.zeros((8, tb), F32)
            r = jnp.full((8, tb), k16, F32)
            for a in range(PEER_TOPK - 1, -1, -1):
                l = jnp.where(s1[k] >= v1[a], bmax[a], l)
                r = jnp.where(s2[k] >= v2[a], float(a), r)
            lim.append(l)
            rank2.append(r)
        c1_ref[h] = 0.5 * jnp.exp(jnp.concatenate(s1, axis=0) - r1[0]) / z
        lim_ref[h] = jnp.concatenate(lim, axis=0)
        p2_ref[h] = jnp.exp(jnp.concatenate(s2, axis=0) - r2[0]).astype(BF16)
        r2_ref[h] = jnp.concatenate(rank2, axis=0).astype(BF16)
        tied = jnp.sum(counts, axis=0, keepdims=True) != k16
        for s, v, r in ((s1, v1, r1), (s2, v2, r2)):
            count = sum(jnp.where(x >= v[-1], 1.0, 0.0) for x in s)
            tied = tied | (jnp.sum(count, axis=0, keepdims=True) != k16)
            for a in range(PEER_TOPK - 1):
                tied = tied | (r[a] == r[a + 1])
        return tied

    def head_exact(h, cand):
        s1 = sc_ref[2 * h]
        s2 = sc_ref[2 * h + 1]
        v1, rank1 = _topk_ranks(s1)
        v2, rank2 = _topk_ranks(s2)
        for i, (a, b) in enumerate(_CAND):
            cand[i:i + 1, :] = v1[a] + v2[b]
        top = v1[0] + v2[0]
        sel = jnp.zeros(cand.shape, F32)
        z = jnp.zeros((1, tb), F32)
        x = cand[...]
        for _ in range(PEER_TOPK):
            m, hit = _pop_max(x, flat, True)
            sel = jnp.where(hit, 1.0, sel)
            x = jnp.where(hit, -jnp.inf, x)
            z = z + jnp.exp(m - top)
        lim = jnp.zeros(s1.shape, F32)
        for a in range(PEER_TOPK):
            bmax = jnp.sum(jnp.where(arow == float(a), sel, 0.0), axis=0, keepdims=True)
            lim = jnp.where(rank1 == float(a), bmax, lim)
        c1_ref[h] = 0.5 * jnp.exp(s1 - v1[0]) / z
        lim_ref[h] = lim
        p2_ref[h] = jnp.exp(s2 - v2[0]).astype(BF16)
        r2_ref[h] = rank2.astype(BF16)

    group = cand_ref.shape[0]

    def head_group(i, carry):
        tied = [head_fast(group * i + j, cand_ref.at[j]) for j in range(group)]
        for j in range(group):
            @pl.when(jnp.max(jnp.where(tied[j], 1.0, 0.0)) > 0.0)
            def _():
                head_exact(group * i + j, cand_ref.at[j])

        return carry

    lax.fori_loop(0, PEER_HEADS // group, head_group, 0)


def _select(sc, tb):
    nsc, nk, t = sc.shape
    flat = np.full((_NCAND_PAD, 1), 1023.0, np.float32)
    arow = np.full((_NCAND_PAD, 1), -1.0, np.float32)
    for i, (a, b) in enumerate(_CAND):
        flat[i, 0] = a * PEER_TOPK + b
        arow[i, 0] = a
    rowsel = jnp.asarray(arow.T == np.arange(PEER_TOPK, dtype=np.float32)[:, None]).astype(BF16)
    flat = jnp.asarray(np.broadcast_to(flat, (_NCAND_PAD, tb)))
    arow = jnp.asarray(np.broadcast_to(arow, (_NCAND_PAD, tb)))
    out = lambda dt: jax.ShapeDtypeStruct((PEER_HEADS, nk, t), dt)
    ospec = pl.BlockSpec((PEER_HEADS, nk, tb), lambda i: (0, 0, i))
    cspec = pl.BlockSpec((_NCAND_PAD, tb), lambda i: (0, 0))
    return pl.pallas_call(
        _select_kernel,
        grid=(t // tb,),
        in_specs=[pl.BlockSpec((nsc, nk, tb), lambda i: (0, 0, i)), cspec, cspec,
                  pl.BlockSpec((PEER_TOPK, _NCAND_PAD), lambda i: (0, 0))],
        out_specs=[ospec, ospec, ospec, ospec],
        out_shape=[out(F32), out(F32), out(BF16), out(BF16)],
        scratch_shapes=[pltpu.VMEM((2, _NCAND_PAD, tb), F32)],
        compiler_params=_cparams("parallel"),
        name="select",
    )(sc, flat, arow, rowsel)


def _experts_kernel(xnt_ref, u_ref, vt_ref, c1_ref, lim_ref, p2_ref, r2_ref, h_ref, gfin_ref, o_ref,
                    acc_ref, g_ref, *, nblk):
    s = pl.program_id(0)
    e = lax.rem(s, nblk)
    rows = c1_ref.shape[1]
    nk = PEER_NKEYS

    slot = lax.rem(s, 2)
    g_new = g_ref.at[slot]
    g_old = g_ref.at[1 - slot]

    @pl.when(s == 0)
    def _():
        acc_ref[...] = jnp.zeros_like(acc_ref)
        g_old[...] = jnp.zeros_like(g_old)

    def row_bf16(ref, h, r):
        x = jnp.broadcast_to(ref[h, r:r + 1, :], (16, ref.shape[2])).astype(BF16)
        return jnp.concatenate([x] * (nk // 16), axis=0)

    mc = min(512, u_ref.shape[0])
    a = jnp.concatenate([_dot(u_ref[i:i + mc, :], xnt_ref[...]) for i in range(0, u_ref.shape[0], mc)], axis=0)
    contrib = _dot(vt_ref[...], g_old[...])
    acc_ref[...] = jnp.where(lax.rem(s + nblk - 1, nblk) == 0, contrib, acc_ref[...] + contrib)
    for r in range(rows):
        w = None
        for h in range(PEER_HEADS):
            term = jnp.where(r2_ref[h] < row_bf16(lim_ref, h, r), p2_ref[h], jnp.zeros((), BF16)) \
                * row_bf16(c1_ref, h, r)
            w = term if w is None else w + term
        ab = a[r * nk:(r + 1) * nk].astype(BF16)
        g_new[r * nk:(r + 1) * nk, :] = ab * (1.0 + lax.erf(ab * INV_SQRT2)) * w

    @pl.when(jnp.logical_and(e == 0, s > 0))
    def _():
        hh = h_ref[...] + acc_ref[...].T
        o_ref[...] = hh * lax.rsqrt(jnp.mean(hh * hh, axis=-1, keepdims=True) + EPS) * gfin_ref[...]


def _experts(xnt, u_bf, vt_blk, c1, lim, p2, r2, h, gfin, tt):
    t, d = h.shape
    nblk, _, et = vt_blk.shape
    rows = et // PEER_NKEYS
    nt = t // tt
    tile = lambda s: jnp.minimum(s // nblk, nt - 1)
    blk = lambda s: lax.rem(s, nblk)
    prev_tile = lambda s: jnp.maximum(s - 1, 0) // nblk
    prev_blk = lambda s: lax.rem(s + nblk - 1, nblk)
    tok3 = pl.BlockSpec((PEER_HEADS, PEER_NKEYS, tt), lambda s: (0, 0, tile(s)))
    row3 = pl.BlockSpec((PEER_HEADS, rows, tt), lambda s: (0, blk(s), tile(s)))
    return pl.pallas_call(
        functools.partial(_experts_kernel, nblk=nblk),
        grid=(nt * nblk + 1,),
        in_specs=[
            pl.BlockSpec((d, tt), lambda s: (0, tile(s))),
            pl.BlockSpec((et, d), lambda s: (blk(s), 0)),
            pl.BlockSpec((None, d, et), lambda s: (prev_blk(s), 0, 0)),
            row3, row3, tok3, tok3,
            pl.BlockSpec((tt, d), lambda s: (prev_tile(s), 0)),
            pl.BlockSpec((1, d), lambda s: (0, 0)),
        ],
        out_specs=pl.BlockSpec((tt, d), lambda s: (prev_tile(s), 0)),
        out_shape=jax.ShapeDtypeStruct((t, d), F32),
        scratch_shapes=[pltpu.VMEM((d, tt), F32), pltpu.VMEM((2, et, tt), BF16)],
        compiler_params=_cparams("arbitrary"),
        name="experts",
    )(xnt, u_bf, vt_blk, c1, lim, p2, r2, h, gfin)


def _rope_tables(seq):
    rows = seq // GRID_W
    row = np.broadcast_to(np.arange(rows)[:, None], (rows, GRID_W)).reshape(-1)
    col = np.broadcast_to(np.arange(GRID_W)[None, :], (rows, GRID_W)).reshape(-1)
    half = HEAD_DIM // 4
    freqs = ROPE_THETA ** (-np.arange(half, dtype=np.float64) / half)

    def cs(pos):
        ang = pos.astype(np.float64)[:, None] * freqs[None, :]
        return np.cos(ang), np.sin(ang)

    cr, sr = cs(row)
    cc, sc = cs(col)
    zero = np.zeros_like(sr)
    cos = np.concatenate([cr, cr, cc, cc], axis=1)
    sa = np.concatenate([-sr, zero, -sc, zero], axis=1)
    sb = np.concatenate([zero, sr, zero, sc], axis=1)
    tile = lambda a: jnp.asarray(np.concatenate([a, a], axis=1).astype(np.float32))
    return tile(cos), tile(sa), tile(sb)


def _bucket_tables():
    qi = np.arange(Q_BLOCK)
    kj = np.arange(3 * Q_BLOCK)
    rel = kj[None, :, None] - Q_BLOCK * np.arange(3)[:, None, None] - qi[None, None, :]
    nbk = N_BUCKETS // 2
    max_exact = nbk // 2
    assert (max_exact, MAX_DISTANCE, nbk - max_exact) == (8, 128, 8), "the integer form below assumes these constants"
    n = np.abs(rel)
    log_part = np.floor(np.log2(np.maximum(n * n // 64, 1))).astype(np.int64)
    log_part = np.where(2 ** (log_part + 1) * 64 <= n * n, log_part + 1, log_part)
    log_part = np.where(2 ** log_part * 64 > n * n, log_part - 1, log_part)
    large = np.minimum(max_exact + log_part, nbk - 1)
    bucket = np.where(rel > 0, nbk, 0) + np.where(n < max_exact, n, large)
    return jnp.asarray(np.where(n <= WINDOW, bucket, N_BUCKETS).astype(np.int32))


def _tile_sizes(seq, tokens):
    tm = min(512, seq)
    tq = min(512, seq)
    tb = min(V7X_LANES, tokens)
    tt = min(512, tokens)
    return tm, tq, tb, tt


def kernel(x, g_mix, w_in, b_gate, q_norm_g, k_norm_g, rel_bias, sink, w_branch_a, w_branch_b, w_out, g_ffn,
           peer_wq, peer_subkeys, peer_u, peer_v, g_final):
    bsz, seq, d = x.shape
    t = bsz * seq
    depth = w_in.shape[0]
    tm, tq, tb, tt = _tile_sizes(seq, t)
    assert depth == 1, "the experts stage fuses the final norm, so only a single layer is supported"
    assert seq % tm == 0 and seq % Q_BLOCK == 0 and t % tb == 0 and t % tt == 0

    cos, sa, sb = _rope_tables(seq)
    bucket_t = _bucket_tables()
    bd = jnp.asarray(np.kron(np.eye(2, dtype=np.float32),
                             np.full((HEAD_DIM, HEAD_DIM), 1.0 / HEAD_DIM, np.float32))).astype(BF16)
    expert_block = 2048

    h = x.reshape(t, d)
    for l in range(depth):
        qg = jnp.tile(q_norm_g[l], 2)[None, :]
        kg = jnp.tile(k_norm_g[l], 2)[None, :]
        qa, ka, va, qbt, kb, vbt, gate = _in_proj(
            h, g_mix[l][None, :], w_in[l].astype(BF16), b_gate[l][None, :], qg, kg, cos, sa, sb, bd, seq, tm)
        oa = _attn_a(qa, ka, va, bsz, seq, tq)
        ob = _attn_b(qbt, kb, vbt, bucket_t, rel_bias, sink[l][None, :], bsz, seq)
        skh, skl = _split_bf16(peer_subkeys[l].reshape(2 * PEER_HEADS, PEER_NKEYS, PEER_DQ // 2))
        sk3 = jnp.concatenate([skh, skh, skl], axis=-1)
        h1, xnt, sc = _merge(oa, ob, gate, h, w_branch_a[l].astype(BF16), w_branch_b[l].astype(BF16),
                            w_out[l].astype(BF16), g_ffn[l][None, :], peer_wq[l].astype(BF16), sk3, tm)
        c1, lim, p2, r2 = _select(sc, tb)
        vt_blk = peer_v[l].astype(BF16).reshape(-1, expert_block, d).transpose(0, 2, 1)
        out = _experts(xnt, peer_u[l].astype(BF16), vt_blk, c1, lim, p2, r2, h1, g_final[None, :], tt)
    return out.reshape(bsz, seq, d)
```

```python
import functools

import numpy as np
import jax
import jax.numpy as jnp
from jax import lax
from jax.experimental import pallas as pl
from jax.experimental.pallas import tpu as pltpu

F32 = jnp.float32
BF16 = jnp.bfloat16

HEAD_DIM = 64
A_HEADS, A_KV = 8, 2
B_HEADS, B_KV = 8, 2
Q_BLOCK = 128
WINDOW = 128
GRID_W = 64
ROPE_THETA = 10000.0
N_BUCKETS = 32
MAX_DISTANCE = 128
PEER_HEADS = 8
PEER_NKEYS = 128
PEER_TOPK = 16
PEER_DQ = 256
EPS = 1e-6
NEG = -1e30
INV_SQRT2 = 0.7071067811865476
LOG2E = 1.4426950408889634

V7X_LANES = 128
V7X_VMEM_LIMIT_BYTES = 56 * 1024 * 1024

_CAND = [(a, b) for a in range(PEER_TOPK) for b in range(PEER_TOPK) if (a + 1) * (b + 1) <= PEER_TOPK]
_NCAND = len(_CAND)
_NCAND_PAD = -(-_NCAND // 8) * 8


def _cparams(*sem):
    return pltpu.CompilerParams(dimension_semantics=sem, vmem_limit_bytes=V7X_VMEM_LIMIT_BYTES)


def _split_bf16(x):
    hi = x.astype(BF16)
    lo = (x - hi.astype(F32)).astype(BF16)
    return hi, lo


def _dot(a, b):
    return jnp.dot(a, b, preferred_element_type=F32)


def _dot_nt(a, b):
    return lax.dot_general(a, b, (((1,), (1,)), ((), ())), preferred_element_type=F32)


def _in_proj_kernel(x_ref, gmix_ref, w_ref, bg_ref, qg_ref, kg_ref, cos_ref, sa_ref, sb_ref, bd_ref,
                    qa_ref, ka_ref, va_ref, qbt_ref, kb_ref, vbt_ref, gate_ref):
    d = x_ref.shape[1]
    x = x_ref[...]
    n = x * lax.rsqrt(jnp.mean(x * x, axis=-1, keepdims=True) + EPS) * gmix_ref[...]
    nb = n.astype(BF16)
    cos, sa, sb, bd = cos_ref[...], sa_ref[...], sb_ref[...], bd_ref[...]

    def norm_rope(q, gain):
        hi, lo = _split_bf16(q * q)
        ms = _dot(hi, bd) + _dot(lo, bd)
        y = q * lax.rsqrt(ms + EPS) * gain
        return y * cos + pltpu.roll(y, V7X_LANES - 16, 1) * sa + pltpu.roll(y, 16, 1) * sb

    def put_heads(ref, first, y):
        ref[first] = y[:, :HEAD_DIM].astype(BF16)
        ref[first + 1] = y[:, HEAD_DIM:].astype(BF16)

    def put_values(ref, y):
        lane = lax.broadcasted_iota(jnp.int32, y.shape, 1)
        tail = jnp.where(lane == HEAD_DIM, 1.0, 0.0)
        ref[0] = jnp.where(lane < HEAD_DIM, y, tail).astype(BF16)
        ref[1] = jnp.where(lane < HEAD_DIM, pltpu.roll(y, HEAD_DIM, 1), tail).astype(BF16)

    qscale = HEAD_DIM ** -0.5 * LOG2E
    n_attn = (A_HEADS + 2 * A_KV + B_HEADS + 2 * B_KV) * HEAD_DIM
    z = _dot(nb, w_ref[:, :n_attn])
    pw = 2 * HEAD_DIM
    col = lambda c: z[:, c * pw:(c + 1) * pw]
    c = 0
    for j in range(A_HEADS // 2):
        put_heads(qa_ref, 2 * j, norm_rope(col(c), qg_ref[...]) * qscale)
        c += 1
    put_heads(ka_ref, 0, norm_rope(col(c), kg_ref[...]))
    put_values(va_ref, col(c + 1))
    c += 2
    nqb = B_HEADS // 2
    qbt_ref[...] = (z[:, c * pw:(c + nqb) * pw] * qscale).T.astype(BF16)
    c += nqb
    put_heads(kb_ref, 0, col(c))
    y = col(c + 1)
    lane = lax.broadcasted_iota(jnp.int32, y.shape, 1)
    tail = jnp.where(lane == HEAD_DIM, 1.0, 0.0)
    vbt_ref[0] = jnp.where(lane < HEAD_DIM, y, tail).T.astype(BF16)
    vbt_ref[1] = jnp.where(lane < HEAD_DIM, pltpu.roll(y, HEAD_DIM, 1), tail).T.astype(BF16)
    for j in range(2 * d // 512):
        lo, hi = n_attn + j * 512, n_attn + (j + 1) * 512
        zg = _dot(nb, w_ref[:, lo:hi]) + bg_ref[:, j * 512:(j + 1) * 512]
        gate_ref[:, j * 512:(j + 1) * 512] = (1.0 / (1.0 + jnp.exp(-zg))).astype(BF16)


def _in_proj(x2, gmix, w_bf, bg, qg, kg, cos, sa, sb, bd, seq, tm):
    t, d = x2.shape
    nt = t // tm
    ns = seq // tm
    const = lambda i: (0, 0)
    pw = 2 * HEAD_DIM
    tab = pl.BlockSpec((tm, pw), lambda i: (i % ns, 0))
    head = lambda nh: pl.BlockSpec((nh, tm, HEAD_DIM), lambda i: (0, i, 0))
    wide = lambda nh: pl.BlockSpec((nh, tm, 2 * HEAD_DIM), lambda i: (0, i, 0))
    return pl.pallas_call(
        _in_proj_kernel,
        grid=(nt,),
        in_specs=[
            pl.BlockSpec((tm, d), lambda i: (i, 0)),
            pl.BlockSpec((1, d), const),
            pl.BlockSpec(w_bf.shape, const),
            pl.BlockSpec((1, 2 * d), const),
            pl.BlockSpec((1, pw), const),
            pl.BlockSpec((1, pw), const),
            tab, tab, tab,
            pl.BlockSpec((pw, pw), const),
        ],
        out_specs=[head(A_HEADS), head(A_KV), wide(A_KV),
                   pl.BlockSpec((B_HEADS * HEAD_DIM, tm), lambda i: (0, i)), head(B_KV),
                   pl.BlockSpec((B_KV, 2 * HEAD_DIM, tm), lambda i: (0, 0, i)),
                   pl.BlockSpec((tm, 2 * d), lambda i: (i, 0))],
        out_shape=[
            jax.ShapeDtypeStruct((A_HEADS, t, HEAD_DIM), BF16),
            jax.ShapeDtypeStruct((A_KV, t, HEAD_DIM), BF16),
            jax.ShapeDtypeStruct((A_KV, t, 2 * HEAD_DIM), BF16),
            jax.ShapeDtypeStruct((B_HEADS * HEAD_DIM, t), BF16),
            jax.ShapeDtypeStruct((B_KV, t, HEAD_DIM), BF16),
            jax.ShapeDtypeStruct((B_KV, 2 * HEAD_DIM, t), BF16),
            jax.ShapeDtypeStruct((t, 2 * d), BF16),
        ],
        compiler_params=_cparams("parallel"),
        name="in_proj",
    )(x2, gmix, w_bf, bg, qg, kg, cos, sa, sb, bd)


def _attn_a_kernel(q_ref, k_ref, v_ref, o_ref, *, kc):
    r, tq, hd = q_ref.shape
    seq = k_ref.shape[0]
    q = q_ref[...].reshape(r * tq, hd)
    m = jnp.full((r * tq, 1), -jnp.inf, F32)
    acc = jnp.zeros((r * tq, v_ref.shape[1]), F32)
    for c in range(seq // kc):
        s = _dot_nt(q, k_ref[c * kc:(c + 1) * kc, :])
        m_new = jnp.maximum(m, jnp.max(s, axis=-1, keepdims=True))
        p = jnp.exp2((s - m_new).astype(BF16))
        acc = jnp.exp2(m - m_new) * acc + _dot(p, v_ref[c * kc:(c + 1) * kc, :])
        m = m_new
    o = acc[:, :hd] / acc[:, hd:hd + 1]
    for j in range(r):
        o_ref[:, j * hd:(j + 1) * hd] = o[j * tq:(j + 1) * tq].astype(BF16)


def _attn_a(qa, ka, va, bsz, seq, tq):
    t = qa.shape[1]
    r = A_HEADS // A_KV
    nq = seq // tq
    return pl.pallas_call(
        functools.partial(_attn_a_kernel, kc=min(512, seq)),
        grid=(bsz, A_KV, nq),
        in_specs=[
            pl.BlockSpec((r, tq, HEAD_DIM), lambda b, g, i: (g, b * nq + i, 0)),
            pl.BlockSpec((None, seq, HEAD_DIM), lambda b, g, i: (g, b, 0)),
            pl.BlockSpec((None, seq, 2 * HEAD_DIM), lambda b, g, i: (g, b, 0)),
        ],
        out_specs=pl.BlockSpec((tq, r * HEAD_DIM), lambda b, g, i: (b * nq + i, g)),
        out_shape=jax.ShapeDtypeStruct((t, A_HEADS * HEAD_DIM), BF16),
        compiler_params=_cparams("parallel", "parallel", "parallel"),
        name="attn_a",
    )(qa, ka, va)


def _attn_b_kernel(relb_ref, sink_ref, qt_ref, k_ref, vt_ref, bucket_t_ref, o_ref, bias_ref):
    b = pl.program_id(0)
    n = pl.program_id(1)
    nblk = pl.num_programs(1)
    r = B_HEADS // B_KV
    qb = Q_BLOCK
    kw = 3 * Q_BLOCK
    hd = HEAD_DIM

    @pl.when(jnp.logical_and(b == 0, n == 0))
    def _():
        for v in range(bucket_t_ref.shape[0]):
            bucket_t = bucket_t_ref[v]
            for h in range(B_HEADS):
                acc = jnp.full((kw, qb), NEG, F32)
                for k in range(N_BUCKETS):
                    acc = jnp.where(bucket_t == k, relb_ref[k, h] * LOG2E, acc)
                bias_ref[v, h // r, :, (h % r) * qb:(h % r + 1) * qb] = acc

    nq = qt_ref.shape[1] // qb
    last = nblk * nq - 1
    head_of_lane = lax.broadcasted_iota(jnp.int32, (1, r * qb), 1) // qb
    for i in range(nq):
        blk = n * nq + i
        variant = jnp.where(blk == 0, 0, jnp.where(blk == last, 2, 1))
        start = pl.multiple_of(jnp.clip(blk - 1, 0, last - 2) * qb, qb)
        for g in range(B_KV):
            kwin = k_ref[g, pl.ds(start, kw), :]
            vtw = vt_ref[g, :, pl.ds(start, kw)]
            qt = jnp.concatenate(
                [qt_ref[(g * r + j) * hd:(g * r + j + 1) * hd, i * qb:(i + 1) * qb] for j in range(r)], axis=1)
            st = _dot(kwin, qt) + bias_ref[variant, g]
            sink = jnp.zeros((1, r * qb), F32)
            for j in range(r):
                sink = jnp.where(head_of_lane == j, sink_ref[0, g * r + j] * LOG2E, sink)
            m = jnp.maximum(jnp.max(st, axis=0, keepdims=True), sink)
            ot = _dot(vtw, jnp.exp2(st - m).astype(BF16))
            ot = ot / (ot[hd:hd + 1] + jnp.exp2(sink - m))
            for j in range(r):
                h = g * r + j
                o_ref[i * qb:(i + 1) * qb, h * hd:(h + 1) * hd] = ot[:, j * qb:(j + 1) * qb].T[:, :hd].astype(BF16)


def _attn_b(qbt, kb, vbt, bucket_t, rel_bias, sink, bsz, seq):
    t = qbt.shape[1]
    nb = seq // Q_BLOCK
    assert nb >= 3, "the shifted edge windows need at least three query blocks per sequence"
    nq = 2 if nb % 2 == 0 else 1
    ns = nb // nq
    r = B_HEADS // B_KV
    return pl.pallas_call(
        _attn_b_kernel,
        grid=(bsz, ns),
        in_specs=[
            pl.BlockSpec(memory_space=pltpu.SMEM),
            pl.BlockSpec(memory_space=pltpu.SMEM),
            pl.BlockSpec((B_HEADS * HEAD_DIM, nq * Q_BLOCK), lambda b, n: (0, b * ns + n)),
            pl.BlockSpec((B_KV, seq, HEAD_DIM), lambda b, n: (0, b, 0)),
            pl.BlockSpec((B_KV, 2 * HEAD_DIM, seq), lambda b, n: (0, 0, b)),
            pl.BlockSpec(bucket_t.shape, lambda b, n: (0, 0, 0)),
        ],
        out_specs=pl.BlockSpec((nq * Q_BLOCK, B_HEADS * HEAD_DIM), lambda b, n: (b * ns + n, 0)),
        out_shape=jax.ShapeDtypeStruct((t, B_HEADS * HEAD_DIM), BF16),
        scratch_shapes=[pltpu.VMEM((bucket_t.shape[0], B_KV, 3 * Q_BLOCK, r * Q_BLOCK), F32)],
        compiler_params=_cparams("arbitrary", "arbitrary"),
        name="attn_b",
    )(rel_bias, sink, qbt, kb, vbt, bucket_t)


def _merge_kernel(oa_ref, ob_ref, gate_ref, x_ref, wa_ref, wb_ref, wo_ref, gffn_ref, wq_ref, sk3_ref,
                  h_ref, xnt_ref, sc_ref):
    d = x_ref.shape[1]
    pa = _dot(oa_ref[...], wa_ref[...])
    pb = _dot(ob_ref[...], wb_ref[...])
    mix = gate_ref[:, :d].astype(F32) * pa + gate_ref[:, d:].astype(F32) * pb
    h = x_ref[...] + _dot(mix.astype(BF16), wo_ref[...])
    h_ref[...] = h
    xn = h * lax.rsqrt(jnp.mean(h * h, axis=-1, keepdims=True) + EPS) * gffn_ref[...]
    xnb = xn.astype(BF16)
    xnt_ref[...] = xn.T.astype(BF16)
    half = PEER_DQ // 2
    qh, ql = _split_bf16(_dot(xnb, wq_ref[...]))
    for hc in range(2 * PEER_HEADS):
        cols = slice(hc * half, (hc + 1) * half)
        q3 = jnp.concatenate([qh[:, cols], ql[:, cols], qh[:, cols]], axis=1)
        sc_ref[hc] = _dot_nt(sk3_ref[hc], q3)


def _merge(oa, ob, gate, x2, wa, wb, wo, gffn, wq, sk3, tm):
    t, d = x2.shape
    const2 = lambda i: (0, 0)
    const3 = lambda i: (0, 0, 0)
    row = lambda w: pl.BlockSpec((tm, w), lambda i: (i, 0))
    nsc = 2 * PEER_HEADS
    return pl.pallas_call(
        _merge_kernel,
        grid=(t // tm,),
        in_specs=[
            row(oa.shape[1]), row(ob.shape[1]), row(2 * d), row(d),
            pl.BlockSpec(wa.shape, const2), pl.BlockSpec(wb.shape, const2), pl.BlockSpec(wo.shape, const2),
            pl.BlockSpec((1, d), const2), pl.BlockSpec(wq.shape, const2),
            pl.BlockSpec(sk3.shape, const3),
        ],
        out_specs=[row(d), pl.BlockSpec((d, tm), lambda i: (0, i)),
                   pl.BlockSpec((nsc, PEER_NKEYS, tm), lambda i: (0, 0, i))],
        out_shape=[
            jax.ShapeDtypeStruct((t, d), F32),
            jax.ShapeDtypeStruct((d, t), BF16),
            jax.ShapeDtypeStruct((nsc, PEER_NKEYS, t), F32),
        ],
        compiler_params=_cparams("parallel"),
        name="merge",
    )(oa, ob, gate, x2, wa, wb, wo, gffn, wq, sk3)


def _pop_max(x, order, exact):
    m = jnp.max(x, axis=0, keepdims=True)
    if exact:
        first = jnp.min(jnp.where(x == m, order, jnp.inf), axis=0, keepdims=True)
        return m, order == first
    return m, x == m


def _topk_ranks(s):
    rows = lax.broadcasted_iota(jnp.int32, s.shape, 0).astype(F32)
    rank = jnp.full(s.shape, float(PEER_TOPK), F32)
    x = s
    vals = []
    for a in range(PEER_TOPK):
        m, hit = _pop_max(x, rows, True)
        rank = jnp.where(hit, float(a), rank)
        x = jnp.where(hit, -jnp.inf, x)
        vals.append(m)
    return vals, rank


def _batcher_layers(n):
    layers = []
    p = 1
    while p < n:
        k = p
        while k >= 1:
            layer = []
            for j in range(k % p, n - k, 2 * k):
                for i in range(min(k, n - j - k)):
                    if (i + j) // (2 * p) == (i + j + k) // (2 * p):
                        layer.append((i + j, i + j + k))
            layers.append(layer)
            k //= 2
        p *= 2
    return layers


_SORT16 = _batcher_layers(PEER_TOPK)


def _exchange(xs, i, j):
    xs[i], xs[j] = jnp.maximum(xs[i], xs[j]), jnp.minimum(xs[i], xs[j])


def _sorted_top16(xs):
    xs = list(xs)
    n = len(xs)
    for layer in _SORT16:
        for i, j in layer:
            _exchange(xs, i, j)
    for shift in (4, 2, 1):
        other = [pltpu.roll(x, shift, 0) for x in xs]
        xs = [jnp.maximum(xs[p], other[n - 1 - p]) for p in range(n)]
        stride = n // 2
        while stride >= 1:
            for base in range(0, n, 2 * stride):
                for i in range(base, base + stride):
                    _exchange(xs, i, i + stride)
            stride //= 2
    return xs


def _select_kernel(sc_ref, flat_ref, arow_ref, rowsel_ref, c1_ref, lim_ref, p2_ref, r2_ref, cand_ref):
    tb = sc_ref.shape[2]
    flat = flat_ref[...]
    arow = arow_ref[...]
    cand_ref[...] = jnp.full(cand_ref.shape, -jnp.inf, F32)
    nv = PEER_NKEYS // 8
    k16 = float(PEER_TOPK)

    def head_fast(h, cand):
        s1 = [sc_ref[2 * h, 8 * k:8 * k + 8, :] for k in range(nv)]
        s2 = [sc_ref[2 * h + 1, 8 * k:8 * k + 8, :] for k in range(nv)]
        v1 = _sorted_top16(s1)
        v2 = _sorted_top16(s2)
        r1 = [v[0:1] for v in v1]
        r2 = [v[0:1] for v in v2]
        for i, (a, b) in enumerate(_CAND):
            cand[i:i + 1, :] = r1[a] + r2[b]
        top = r1[0] + r2[0]
        sums = cand[...]
        x = sums
        for _ in range(PEER_TOPK):
            m, hit = _pop_max(x, flat, False)
            x = jnp.where(hit, -jnp.inf, x)
        sel = sums >= m
        z = jnp.sum(jnp.where(sel, jnp.exp(sums - top), 0.0), axis=0, keepdims=True)
        counts = _dot(rowsel_ref[...], jnp.where(sel, 1.0, 0.0).astype(BF16))
        bmax = [jnp.broadcast_to(counts[a:a + 1], (8, tb)) for a in range(PEER_TOPK)]
        lim, rank2 = [], []
        for k in range(nv):
            l = jnp.zeros((8, tb), F32)
            r = jnp.full((8, tb), k16, F32)
            for a in range(PEER_TOPK - 1, -1, -1):
                l = jnp.where(s1[k] >= v1[a], bmax[a], l)
                r = jnp.where(s2[k] >= v2[a], float(a), r)
            lim.append(l)
            rank2.append(r)
        c1_ref[h] = 0.5 * jnp.exp(jnp.concatenate(s1, axis=0) - r1[0]) / z
        lim_ref[h] = jnp.concatenate(lim, axis=0)
        p2_ref[h] = jnp.exp(jnp.concatenate(s2, axis=0) - r2[0]).astype(BF16)
        r2_ref[h] = jnp.concatenate(rank2, axis=0).astype(BF16)
        tied = jnp.sum(counts, axis=0, keepdims=True) != k16
        for s, v, r in ((s1, v1, r1), (s2, v2, r2)):
            count = sum(jnp.where(x >= v[-1], 1.0, 0.0) for x in s)
            tied = tied | (jnp.sum(count, axis=0, keepdims=True) != k16)
            for a in range(PEER_TOPK - 1):
                tied = tied | (r[a] == r[a + 1])
        return tied

    def head_exact(h, cand):
        s1 = sc_ref[2 * h]
        s2 = sc_ref[2 * h + 1]
        v1, rank1 = _topk_ranks(s1)
        v2, rank2 = _topk_ranks(s2)
        for i, (a, b) in enumerate(_CAND):
            cand[i:i + 1, :] = v1[a] + v2[b]
        top = v1[0] + v2[0]
        sel = jnp.zeros(cand.shape, F32)
        z = jnp.zeros((1, tb), F32)
        x = cand[...]
        for _ in range(PEER_TOPK):
            m, hit = _pop_max(x, flat, True)
            sel = jnp.where(hit, 1.0, sel)
            x = jnp.where(hit, -jnp.inf, x)
            z = z + jnp.exp(m - top)
        lim = jnp.zeros(s1.shape, F32)
        for a in range(PEER_TOPK):
            bmax = jnp.sum(jnp.where(arow == float(a), sel, 0.0), axis=0, keepdims=True)
            lim = jnp.where(rank1 == float(a), bmax, lim)
        c1_ref[h] = 0.5 * jnp.exp(s1 - v1[0]) / z
        lim_ref[h] = lim
        p2_ref[h] = jnp.exp(s2 - v2[0]).astype(BF16)
        r2_ref[h] = rank2.astype(BF16)

    group = cand_ref.shape[0]

    def head_group(i, carry):
        tied = [head_fast(group * i + j, cand_ref.at[j]) for j in range(group)]
        for j in range(group):
            @pl.when(jnp.max(jnp.where(tied[j], 1.0, 0.0)) > 0.0)
            def _():
                head_exact(group * i + j, cand_ref.at[j])

        return carry

    lax.fori_loop(0, PEER_HEADS // group, head_group, 0)


def _select(sc, tb):
    nsc, nk, t = sc.shape
    flat = np.full((_NCAND_PAD, 1), 1023.0, np.float32)
    arow = np.full((_NCAND_PAD, 1), -1.0, np.float32)
    for i, (a, b) in enumerate(_CAND):
        flat[i, 0] = a * PEER_TOPK + b
        arow[i, 0] = a
    rowsel = jnp.asarray(arow.T == np.arange(PEER_TOPK, dtype=np.float32)[:, None]).astype(BF16)
    flat = jnp.asarray(np.broadcast_to(flat, (_NCAND_PAD, tb)))
    arow = jnp.asarray(np.broadcast_to(arow, (_NCAND_PAD, tb)))
    out = lambda dt: jax.ShapeDtypeStruct((PEER_HEADS, nk, t), dt)
    ospec = pl.BlockSpec((PEER_HEADS, nk, tb), lambda i: (0, 0, i))
    cspec = pl.BlockSpec((_NCAND_PAD, tb), lambda i: (0, 0))
    return pl.pallas_call(
        _select_kernel,
        grid=(t // tb,),
        in_specs=[pl.BlockSpec((nsc, nk, tb), lambda i: (0, 0, i)), cspec, cspec,
                  pl.BlockSpec((PEER_TOPK, _NCAND_PAD), lambda i: (0, 0))],
        out_specs=[ospec, ospec, ospec, ospec],
        out_shape=[out(F32), out(F32), out(BF16), out(BF16)],
        scratch_shapes=[pltpu.VMEM((2, _NCAND_PAD, tb), F32)],
        compiler_params=_cparams("parallel"),
        name="select",
    )(sc, flat, arow, rowsel)


def _experts_kernel(xnt_ref, u_ref, vt_ref, c1_ref, lim_ref, p2_ref, r2_ref, h_ref, gfin_ref, o_ref,
                    acc_ref, g_ref, *, nblk):
    s = pl.program_id(0)
    e = lax.rem(s, nblk)
    rows = c1_ref.shape[1]
    nk = PEER_NKEYS

    slot = lax.rem(s, 2)
    g_new = g_ref.at[slot]
    g_old = g_ref.at[1 - slot]

    @pl.when(s == 0)
    def _():
        acc_ref[...] = jnp.zeros_like(acc_ref)
        g_old[...] = jnp.zeros_like(g_old)

    def row_bf16(ref, h, r):
        x = jnp.broadcast_to(ref[h, r:r + 1, :], (16, ref.shape[2])).astype(BF16)
        return jnp.concatenate([x] * (nk // 16), axis=0)

    mc = min(512, u_ref.shape[0])
    a = jnp.concatenate([_dot(u_ref[i:i + mc, :], xnt_ref[...]) for i in range(0, u_ref.shape[0], mc)], axis=0)
    contrib = _dot(vt_ref[...], g_old[...])
    acc_ref[...] = jnp.where(lax.rem(s + nblk - 1, nblk) == 0, contrib, acc_ref[...] + contrib)
    for r in range(rows):
        w = None
        for h in range(PEER_HEADS):
            term = jnp.where(r2_ref[h] < row_bf16(lim_ref, h, r), p2_ref[h], jnp.zeros((), BF16)) \
                * row_bf16(c1_ref, h, r)
            w = term if w is None else w + term
        ab = a[r * nk:(r + 1) * nk].astype(BF16)
        g_new[r * nk:(r + 1) * nk, :] = ab * (1.0 + lax.erf(ab * INV_SQRT2)) * w

    @pl.when(jnp.logical_and(e == 0, s > 0))
    def _():
        hh = h_ref[...] + acc_ref[...].T
        o_ref[...] = hh * lax.rsqrt(jnp.mean(hh * hh, axis=-1, keepdims=True) + EPS) * gfin_ref[...]


def _experts(xnt, u_bf, vt_blk, c1, lim, p2, r2, h, gfin, tt):
    t, d = h.shape
    nblk, _, et = vt_blk.shape
    rows = et // PEER_NKEYS
    nt = t // tt
    tile = lambda s: jnp.minimum(s // nblk, nt - 1)
    blk = lambda s: lax.rem(s, nblk)
    prev_tile = lambda s: jnp.maximum(s - 1, 0) // nblk
    prev_blk = lambda s: lax.rem(s + nblk - 1, nblk)
    tok3 = pl.BlockSpec((PEER_HEADS, PEER_NKEYS, tt), lambda s: (0, 0, tile(s)))
    row3 = pl.BlockSpec((PEER_HEADS, rows, tt), lambda s: (0, blk(s), tile(s)))
    return pl.pallas_call(
        functools.partial(_experts_kernel, nblk=nblk),
        grid=(nt * nblk + 1,),
        in_specs=[
            pl.BlockSpec((d, tt), lambda s: (0, tile(s))),
            pl.BlockSpec((et, d), lambda s: (blk(s), 0)),
            pl.BlockSpec((None, d, et), lambda s: (prev_blk(s), 0, 0)),
            row3, row3, tok3, tok3,
            pl.BlockSpec((tt, d), lambda s: (prev_tile(s), 0)),
            pl.BlockSpec((1, d), lambda s: (0, 0)),
        ],
        out_specs=pl.BlockSpec((tt, d), lambda s: (prev_tile(s), 0)),
        out_shape=jax.ShapeDtypeStruct((t, d), F32),
        scratch_shapes=[pltpu.VMEM((d, tt), F32), pltpu.VMEM((2, et, tt), BF16)],
        compiler_params=_cparams("arbitrary"),
        name="experts",
    )(xnt, u_bf, vt_blk, c1, lim, p2, r2, h, gfin)


def _rope_tables(seq):
    rows = seq // GRID_W
    row = np.broadcast_to(np.arange(rows)[:, None], (rows, GRID_W)).reshape(-1)
    col = np.broadcast_to(np.arange(GRID_W)[None, :], (rows, GRID_W)).reshape(-1)
    half = HEAD_DIM // 4
    freqs = ROPE_THETA ** (-np.arange(half, dtype=np.float64) / half)

    def cs(pos):
        ang = pos.astype(np.float64)[:, None] * freqs[None, :]
        return np.cos(ang), np.sin(ang)

    cr, sr = cs(row)
    cc, sc = cs(col)
    zero = np.zeros_like(sr)
    cos = np.concatenate([cr, cr, cc, cc], axis=1)
    sa = np.concatenate([-sr, zero, -sc, zero], axis=1)
    sb = np.concatenate([zero, sr, zero, sc], axis=1)
    tile = lambda a: jnp.asarray(np.concatenate([a, a], axis=1).astype(np.float32))
    return tile(cos), tile(sa), tile(sb)


def _bucket_tables():
    qi = np.arange(Q_BLOCK)
    kj = np.arange(3 * Q_BLOCK)
    rel = kj[None, :, None] - Q_BLOCK * np.arange(3)[:, None, None] - qi[None, None, :]
    nbk = N_BUCKETS // 2
    max_exact = nbk // 2
    assert (max_exact, MAX_DISTANCE, nbk - max_exact) == (8, 128, 8), "the integer form below assumes these constants"
    n = np.abs(rel)
    log_part = np.floor(np.log2(np.maximum(n * n // 64, 1))).astype(np.int64)
    log_part = np.where(2 ** (log_part + 1) * 64 <= n * n, log_part + 1, log_part)
    log_part = np.where(2 ** log_part * 64 > n * n, log_part - 1, log_part)
    large = np.minimum(max_exact + log_part, nbk - 1)
    bucket = np.where(rel > 0, nbk, 0) + np.where(n < max_exact, n, large)
    return jnp.asarray(np.where(n <= WINDOW, bucket, N_BUCKETS).astype(np.int32))


def _tile_sizes(seq, tokens):
    tm = min(512, seq)
    tq = min(512, seq)
    tb = min(V7X_LANES, tokens)
    tt = min(512, tokens)
    return tm, tq, tb, tt


def kernel(x, g_mix, w_in, b_gate, q_norm_g, k_norm_g, rel_bias, sink, w_branch_a, w_branch_b, w_out, g_ffn,
           peer_wq, peer_subkeys, peer_u, peer_v, g_final):
    bsz, seq, d = x.shape
    t = bsz * seq
    depth = w_in.shape[0]
    tm, tq, tb, tt = _tile_sizes(seq, t)
    assert depth == 1, "the experts stage fuses the final norm, so only a single layer is supported"
    assert seq % tm == 0 and seq % Q_BLOCK == 0 and t % tb == 0 and t % tt == 0

    cos, sa, sb = _rope_tables(seq)
    bucket_t = _bucket_tables()
    bd = jnp.asarray(np.kron(np.eye(2, dtype=np.float32),
                             np.full((HEAD_DIM, HEAD_DIM), 1.0 / HEAD_DIM, np.float32))).astype(BF16)
    expert_block = 2048

    h = x.reshape(t, d)
    for l in range(depth):
        qg = jnp.tile(q_norm_g[l], 2)[None, :]
        kg = jnp.tile(k_norm_g[l], 2)[None, :]
        qa, ka, va, qbt, kb, vbt, gate = _in_proj(
            h, g_mix[l][None, :], w_in[l].astype(BF16), b_gate[l][None, :], qg, kg, cos, sa, sb, bd, seq, tm)
        oa = _attn_a(qa, ka, va, bsz, seq, tq)
        ob = _attn_b(qbt, kb, vbt, bucket_t, rel_bias, sink[l][None, :], bsz, seq)
        skh, skl = _split_bf16(peer_subkeys[l].reshape(2 * PEER_HEADS, PEER_NKEYS, PEER_DQ // 2))
        sk3 = jnp.concatenate([skh, skh, skl], axis=-1)
        h1, xnt, sc = _merge(oa, ob, gate, h, w_branch_a[l].astype(BF16), w_branch_b[l].astype(BF16),
                            w_out[l].astype(BF16), g_ffn[l][None, :], peer_wq[l].astype(BF16), sk3, tm)
        c1, lim, p2, r2 = _select(sc, tb)
        vt_blk = peer_v[l].astype(BF16).reshape(-1, expert_block, d).transpose(0, 2, 1)
        out = _experts(xnt, peer_u[l].astype(BF16), vt_blk, c1, lim, p2, r2, h1, g_final[None, :], tt)
    return out.reshape(bsz, seq, d)
```
